```python
import jax, jax.numpy as jnp
from jax import lax
import numpy as np

D_MODEL = 4096
BATCH = 2
SEQ = 8192
DEPTH = 2

GRID_W = 64
CTX_LEN = 256
HEAD_DIM = 128
A_HEADS = 16
A_KV_HEADS = 4
WINDOW = 128
BLOCK = 128
B_HEADS = 16
Q_LORA = 1024
KV_LORA = 512
NOPE_DIM = 128
ROPE_DIM = 64
V_DIM = 128
C_HEADS = 16
NA_ROWS = 8
NA_COLS = 16
NA_QCB = NA_COLS
NA_STRIP = 2 * NA_COLS
N_BRANCH = 3
BRANCH_W = A_HEADS * HEAD_DIM
D_FF = ((8 * D_MODEL // 3 + 255) // 256) * 256
KV_COLS = 2 * A_KV_HEADS * HEAD_DIM + KV_LORA + ROPE_DIM + 2 * C_HEADS * HEAD_DIM
Q_COLS = A_HEADS * HEAD_DIM + Q_LORA + C_HEADS * HEAD_DIM
GATE_COLS = N_BRANCH * D_MODEL
IN_COLS = KV_COLS + Q_COLS + GATE_COLS
ROPE_BASE = 10000.0
EPS = 1e-6
NEG_INF = -1e30

kernel_name = 'hybrid_parallel_gated_dit_block'


def _split(p, sizes):
    offs = np.cumsum(sizes)[:-1].tolist()
    return jnp.split(p, offs, axis=-1)


def _rmsnorm(x, g):
    xf = x.astype(jnp.float32)
    y = xf * lax.rsqrt(jnp.mean(xf * xf, axis=-1, keepdims=True) + EPS)
    return (y * g.astype(jnp.float32)).astype(x.dtype)


def _modulate(h, shift, scale):
    return h * (1 + scale) + shift


def _adaln(cvec, w_ada, b_ada):
    m = jax.nn.silu(cvec) @ w_ada + b_ada
    return jnp.split(m, 6, axis=-1)


def _axial_rope(seq, rot_dim, dtype):
    t = jnp.arange(seq, dtype=jnp.int32)
    row = (t // GRID_W).astype(jnp.float32)
    col = (t % GRID_W).astype(jnp.float32)
    n_freq = rot_dim // 4
    inv = jnp.power(ROPE_BASE, -jnp.arange(n_freq, dtype=jnp.float32) / n_freq)
    ang = jnp.concatenate([row[:, None] * inv, col[:, None] * inv], axis=-1)
    return jnp.cos(ang).astype(dtype), jnp.sin(ang).astype(dtype)


def _rope(x, cos, sin):
    x1, x2 = jnp.split(x, 2, axis=-1)
    c = cos[None, :, None, :]
    s = sin[None, :, None, :]
    return jnp.concatenate([x1 * c - x2 * s, x1 * s + x2 * c], axis=-1)


def _kv_parts(p_kv, g_kv_a):
    B, T, _ = p_kv.shape
    a_k, a_v, b_ckv, b_kr, c_k, c_v = _split(
        p_kv, [A_KV_HEADS * HEAD_DIM] * 2 + [KV_LORA, ROPE_DIM] + [C_HEADS * HEAD_DIM] * 2)
    return (a_k.reshape(B, T, A_KV_HEADS, HEAD_DIM), a_v.reshape(B, T, A_KV_HEADS, HEAD_DIM),
            _rmsnorm(b_ckv, g_kv_a), b_kr,
            c_k.reshape(B, T, C_HEADS, HEAD_DIM), c_v.reshape(B, T, C_HEADS, HEAD_DIM))


def _q_parts(p_q, g_q_a, w_q_b):
    B, T, _ = p_q.shape
    a_q, b_qa, c_q = _split(p_q, [A_HEADS * HEAD_DIM, Q_LORA, C_HEADS * HEAD_DIM])
    b_q = jnp.einsum('btr,rhe->bthe', _rmsnorm(b_qa, g_q_a), w_q_b)
    return (a_q.reshape(B, T, A_HEADS, HEAD_DIM), b_q[..., :NOPE_DIM], b_q[..., NOPE_DIM:],
            c_q.reshape(B, T, C_HEADS, HEAD_DIM))


def _joint_softmax(*logits):
    s = jnp.concatenate([l.astype(jnp.float32) for l in logits], axis=-1)
    return jax.nn.softmax(s, axis=-1)


def _dense_gqa(q, k, v, sink=None):
    B, T, H, HD = q.shape
    KVH = k.shape[2]
    G = H // KVH
    qg = q.reshape(B, T, KVH, G, HD)
    s = jnp.einsum('bthgd,blhd->bhgtl', qg, k) * (HD ** -0.5)
    if sink is None:
        p = jax.nn.softmax(s.astype(jnp.float32), axis=-1)
    else:
        sk = jnp.broadcast_to(sink.reshape(1, KVH, G, 1, 1).astype(jnp.float32), s.shape[:-1] + (1,))
        p = _joint_softmax(s, sk)[..., :-1]
    o = jnp.einsum('bhgtl,blhd->bthgd', p.astype(v.dtype), v)
    return o.reshape(B, T, H * HD)


def _window_attention(q, k, v, k_ctx, v_ctx, sink):
    B, S, H, HD = q.shape
    KVH = k.shape[2]
    G = H // KVH
    nb = S // BLOCK
    qb = q.reshape(B, nb, BLOCK, KVH, G, HD)
    pad = ((0, 0), (BLOCK, BLOCK), (0, 0), (0, 0))

    def band(t):
        tp = jnp.pad(t, pad).reshape(B, nb + 2, BLOCK, KVH, HD)
        return jnp.concatenate([tp[:, :-2], tp[:, 1:-1], tp[:, 2:]], axis=2)

    kb, vb = band(k), band(v)
    scale = HD ** -0.5
    s_loc = jnp.einsum('bnqhgd,bnkhd->bnhgqk', qb, kb) * scale
    s_ctx = jnp.einsum('bnqhgd,blhd->bnhgql', qb, k_ctx) * scale
    blk = jnp.arange(nb)[:, None, None]
    qpos = blk * BLOCK + jnp.arange(BLOCK)[None, :, None]
    kpos = (blk - 1) * BLOCK + jnp.arange(3 * BLOCK)[None, None, :]
    valid = (jnp.abs(kpos - qpos) <= WINDOW) & (kpos >= 0) & (kpos < S)
    s_loc = jnp.where(valid[None, :, None, None], s_loc.astype(jnp.float32), NEG_INF)
    sk = jnp.broadcast_to(sink.reshape(1, 1, KVH, G, 1, 1).astype(jnp.float32), s_ctx.shape[:-1] + (1,))
    p = _joint_softmax(s_loc, s_ctx, sk)
    K = 3 * BLOCK
    L = k_ctx.shape[1]
    p_loc = p[..., :K].astype(v.dtype)
    p_ctx = p[..., K:K + L].astype(v.dtype)
    o = (jnp.einsum('bnhgqk,bnkhd->bnqhgd', p_loc, vb)
         + jnp.einsum('bnhgql,blhd->bnqhgd', p_ctx, v_ctx))
    return o.reshape(B, S, H * HD)


def _mla_attend(qn, qr, ckv, kr, w_uk, w_uv):
    B, T, H, _ = qn.shape
    q_lat = jnp.einsum('bqhn,chn->bqhc', qn, w_uk)
    s = (jnp.einsum('bqhc,bkc->bhqk', q_lat, ckv)
         + jnp.einsum('bqhr,bkr->bhqk', qr, kr)) * ((NOPE_DIM + ROPE_DIM) ** -0.5)
    p = jax.nn.softmax(s.astype(jnp.float32), axis=-1).astype(ckv.dtype)
    o_lat = jnp.einsum('bhqk,bkc->bqhc', p, ckv)
    o = jnp.einsum('bqhc,chv->bqhv', o_lat, w_uv)
    return o.reshape(B, T, H * V_DIM)


def _mla_latent(qn, qr, ckv_all, kr_all, w_uk, w_uv):
    B, S, H, _ = qn.shape
    nb = S // BLOCK
    qn_b = qn.reshape(B, nb, BLOCK, H, NOPE_DIM).transpose(1, 0, 2, 3, 4)
    qr_b = qr.reshape(B, nb, BLOCK, H, ROPE_DIM).transpose(1, 0, 2, 3, 4)
    o = lax.map(lambda a: _mla_attend(a[0], a[1], ckv_all, kr_all, w_uk, w_uv), (qn_b, qr_b))
    return o.transpose(1, 0, 2, 3).reshape(B, S, H * V_DIM)


def _na_col_tables():
    ncb = GRID_W // NA_QCB
    starts = np.clip(np.arange(ncb) * NA_QCB - NA_COLS // 2, 0, GRID_W - NA_STRIP)
    qcol = np.arange(ncb)[:, None] * NA_QCB + np.arange(NA_QCB)[None, :]
    kcol = starts[:, None] + np.arange(NA_STRIP)[None, :]
    c0 = np.clip(qcol - NA_COLS // 2, 0, GRID_W - NA_COLS)
    valid = (kcol[:, None, :] >= c0[:, :, None]) & (kcol[:, None, :] < c0[:, :, None] + NA_COLS)
    off = np.clip(kcol[:, None, :] - qcol[:, :, None] + NA_COLS - 1, 0, 2 * NA_COLS - 2)
    return kcol, off, valid


def _neighborhood_attention(q, k, v, k_ctx, v_ctx, rpb):
    B, S, H, HD = q.shape
    rows = S // GRID_W
    kr = min(NA_ROWS, rows)
    ncb = GRID_W // NA_QCB
    K = kr * NA_STRIP
    kcol, off, col_valid = _na_col_tables()
    cidx = jnp.asarray(kcol)
    coff = jnp.asarray(off)
    valid = jnp.asarray(np.broadcast_to(col_valid[:, :, None, :], (ncb, NA_QCB, kr, NA_STRIP))
                        .reshape(ncb, NA_QCB, K))
    kg = k.reshape(B, rows, GRID_W, H, HD)
    vg = v.reshape(B, rows, GRID_W, H, HD)
    qg = q.reshape(B, rows, ncb, NA_QCB, H, HD).transpose(1, 0, 2, 3, 4, 5)
    scale = HD ** -0.5

    def row_block(args):
        r, q_blk = args
        r0 = jnp.clip(r - kr // 2, 0, rows - kr)
        k_rows = lax.dynamic_slice_in_dim(kg, r0, kr, axis=1)
        v_rows = lax.dynamic_slice_in_dim(vg, r0, kr, axis=1)
        k_nb = k_rows[:, :, cidx].transpose(0, 2, 1, 3, 4, 5).reshape(B, ncb, K, H, HD)
        v_nb = v_rows[:, :, cidx].transpose(0, 2, 1, 3, 4, 5).reshape(B, ncb, K, H, HD)
        roff = r0 + jnp.arange(kr) - r + (NA_ROWS - 1)
        bias = rpb[:, roff][:, :, coff]
        bias = bias.transpose(0, 2, 3, 1, 4).reshape(H, ncb, NA_QCB, K)
        s_loc = (jnp.einsum('bjqhd,bjkhd->bhjqk', q_blk, k_nb) * scale).astype(jnp.float32) \
            + bias[None].astype(jnp.float32)
        s_loc = jnp.where(valid, s_loc, NEG_INF)
        s_ctx = jnp.einsum('bjqhd,blhd->bhjql', q_blk, k_ctx) * scale
        p = _joint_softmax(s_loc, s_ctx)
        p_loc = p[..., :K].astype(v.dtype)
        p_ctx = p[..., K:].astype(v.dtype)
        o = (jnp.einsum('bhjqk,bjkhd->bjqhd', p_loc, v_nb)
             + jnp.einsum('bhjql,blhd->bjqhd', p_ctx, v_ctx))
        return o.reshape(B, GRID_W, H * HD)

    o = lax.map(row_block, (jnp.arange(rows, dtype=jnp.int32), qg))
    return o.transpose(1, 0, 2, 3).reshape(B, S, H * HD)


def _merge(gate_logits, o_a, o_b, o_c, w_br, w_out):
    g_a, g_b, g_c = _split(jax.nn.sigmoid(gate_logits.astype(jnp.float32)).astype(o_a.dtype),
                           [D_MODEL] * N_BRANCH)
    y = g_a * (o_a @ w_br[0]) + g_b * (o_b @ w_br[1]) + g_c * (o_c @ w_br[2])
    return y @ w_out


def _swiglu(h, w_gate, w_up, w_down):
    return (jax.nn.silu(h @ w_gate) * (h @ w_up)) @ w_down


def _layer(x, ctx, c, c_ctx, cos_a, sin_a, cos_b, sin_b, w_ada, b_ada, g_mix, w_in, sink_a,
           g_q_a, w_q_b, g_kv_a, w_kv_b, rpb_c, w_br, w_out, g_ffn, w_ffn_gate, w_ffn_up,
           w_ffn_down, last):
    sh_m, sc_m, gt_m, sh_f, sc_f, gt_f = [m[:, None, :] for m in _adaln(c, w_ada, b_ada)]
    csh_m, csc_m, cgt_m, csh_f, csc_f, cgt_f = _adaln(c_ctx, w_ada, b_ada)
    w_uk, w_uv = w_kv_b[..., :NOPE_DIM], w_kv_b[..., NOPE_DIM:]
    hx = _modulate(_rmsnorm(x, g_mix), sh_m, sc_m)
    hc = _modulate(_rmsnorm(ctx, g_mix), csh_m, csc_m)

    pc = hc @ (w_in[:, :KV_COLS] if last else w_in)
    ak_c, av_c, ckv_c, kr_c, ck_c, cv_c = _kv_parts(pc[..., :KV_COLS], g_kv_a)

    px = hx @ w_in
    ak, av, ckv, kr, ck, cv = _kv_parts(px[..., :KV_COLS], g_kv_a)
    aq, bqn, bqr, cq = _q_parts(px[..., KV_COLS:KV_COLS + Q_COLS], g_q_a, w_q_b)
    aq = _rope(aq, cos_a, sin_a)
    ak = _rope(ak, cos_a, sin_a)
    bqr = _rope(bqr, cos_b, sin_b)
    kr = _rope(kr[:, :, None], cos_b, sin_b)[:, :, 0]
    o_a = _window_attention(aq, ak, av, ak_c, av_c, sink_a)
    o_b = _mla_latent(bqn, bqr, jnp.concatenate([ckv_c, ckv], axis=1),
                      jnp.concatenate([kr_c, kr], axis=1), w_uk, w_uv)
    o_c = _neighborhood_attention(cq, ck, cv, ck_c, cv_c, rpb_c)
    x = x + gt_m * _merge(px[..., KV_COLS + Q_COLS:], o_a, o_b, o_c, w_br, w_out)
    x = x + gt_f * _swiglu(_modulate(_rmsnorm(x, g_ffn), sh_f, sc_f), w_ffn_gate, w_ffn_up, w_ffn_down)

    if not last:
        aq_c, bqn_c, bqr_c, cq_c = _q_parts(pc[..., KV_COLS:KV_COLS + Q_COLS], g_q_a, w_q_b)
        oc_a = _dense_gqa(aq_c, ak_c, av_c, sink_a)
        oc_b = _mla_attend(bqn_c, bqr_c, ckv_c, kr_c, w_uk, w_uv)
        oc_c = _dense_gqa(cq_c, ck_c, cv_c)
        ctx = ctx + cgt_m * _merge(pc[..., KV_COLS + Q_COLS:], oc_a, oc_b, oc_c, w_br, w_out)
        ctx = ctx + cgt_f * _swiglu(_modulate(_rmsnorm(ctx, g_ffn), csh_f, csc_f),
                                    w_ffn_gate, w_ffn_up, w_ffn_down)
    return x, ctx


def setup_inputs(seed: int = 0) -> dict:
    key = jax.random.key(seed)
    ks = jax.random.split(key, 21)
    L = DEPTH

    def nrm(k, shape, s):
        return jax.random.normal(k, shape, jnp.float32) * s

    return {
        'x': nrm(ks[0], (BATCH, SEQ, D_MODEL), 1.0),
        'c': nrm(ks[1], (BATCH, D_MODEL), 1.0),
        'ctx': nrm(ks[2], (BATCH, CTX_LEN, D_MODEL), 1.0),
        'c_ctx': nrm(ks[3], (D_MODEL,), 1.0),
        'w_ada': nrm(ks[4], (L, D_MODEL, 6 * D_MODEL), 0.5 * D_MODEL ** -0.5),
        'b_ada': nrm(ks[5], (L, 6 * D_MODEL), 0.01),
        'g_mix': 1.0 + nrm(ks[6], (L, D_MODEL), 0.02),
        'w_in': nrm(ks[7], (L, D_MODEL, IN_COLS), D_MODEL ** -0.5),
        'sink_a': nrm(ks[8], (L, A_HEADS), 0.5),
        'g_q_a': 1.0 + nrm(ks[9], (L, Q_LORA), 0.02),
        'w_q_b': nrm(ks[10], (L, Q_LORA, B_HEADS, NOPE_DIM + ROPE_DIM), Q_LORA ** -0.5),
        'g_kv_a': 1.0 + nrm(ks[11], (L, KV_LORA), 0.02),
        'w_kv_b': nrm(ks[12], (L, KV_LORA, B_HEADS, NOPE_DIM + V_DIM), KV_LORA ** -0.5),
        'rpb_c': nrm(ks[13], (L, C_HEADS, 2 * NA_ROWS - 1, 2 * NA_COLS - 1), 0.1),
        'w_br': nrm(ks[14], (L, N_BRANCH, BRANCH_W, D_MODEL), BRANCH_W ** -0.5),
        'w_out': nrm(ks[15], (L, D_MODEL, D_MODEL), D_MODEL ** -0.5),
        'g_ffn': 1.0 + nrm(ks[16], (L, D_MODEL), 0.02),
        'w_ffn_gate': nrm(ks[17], (L, D_MODEL, D_FF), D_MODEL ** -0.5),
        'w_ffn_up': nrm(ks[18], (L, D_MODEL, D_FF), D_MODEL ** -0.5),
        'w_ffn_down': nrm(ks[19], (L, D_FF, D_MODEL), D_FF ** -0.5),
        'g_final': 1.0 + nrm(ks[20], (D_MODEL,), 0.02),
    }


def reference(x, c, ctx, c_ctx, w_ada, b_ada, g_mix, w_in, sink_a, g_q_a, w_q_b, g_kv_a, w_kv_b,
              rpb_c, w_br, w_out, g_ffn, w_ffn_gate, w_ffn_up, w_ffn_down, g_final):
    S = x.shape[1]
    cos_a, sin_a = _axial_rope(S, HEAD_DIM, x.dtype)
    cos_b, sin_b = _axial_rope(S, ROPE_DIM, x.dtype)
    for l in range(DEPTH):
        x, ctx = _layer(x, ctx, c, c_ctx, cos_a, sin_a, cos_b, sin_b, w_ada[l], b_ada[l], g_mix[l],
                        w_in[l], sink_a[l], g_q_a[l], w_q_b[l], g_kv_a[l], w_kv_b[l], rpb_c[l],
                        w_br[l], w_out[l], g_ffn[l], w_ffn_gate[l], w_ffn_up[l], w_ffn_down[l],
                        last=(l == DEPTH - 1))
    return _rmsnorm(x, g_final)
```

```python
import functools

import numpy as np
import jax
import jax.numpy as jnp
from jax import lax
from jax.experimental import pallas as pl
from jax.experimental.pallas import tpu as pltpu

D_MODEL = 4096
BATCH = 2
SEQ = 8192
DEPTH = 2
GRID_W = 64
CTX_LEN = 256
HEAD_DIM = 128
A_HEADS = 16
A_KV_HEADS = 4
WINDOW = 128
BLOCK = 128
B_HEADS = 16
Q_LORA = 1024
KV_LORA = 512
NOPE_DIM = 128
ROPE_DIM = 64
V_DIM = 128
C_HEADS = 16
NA_ROWS = 8
NA_COLS = 16
N_BRANCH = 3
ROPE_BASE = 10000.0
EPS = 1e-6
NEG_INF = -1e30

LANE = 128
MLA_QK = 2 * LANE
NA_QROWS = 4
VMEM_LIMIT = 56 * 1024 * 1024

BF16 = jnp.bfloat16
F32 = jnp.float32


def _params(n_axes):
    return pltpu.CompilerParams(dimension_semantics=("parallel",) * n_axes,
                                vmem_limit_bytes=VMEM_LIMIT)


def _dot(a, b):
    return jnp.dot(a, b, preferred_element_type=F32)


def _dot_nt(a, b):
    return lax.dot_general(a, b, (((1,), (1,)), ((), ())), preferred_element_type=F32)


def _tile(n, pref):
    if n <= pref:
        return n
    t = pref - pref % LANE
    while t >= LANE:
        if n % t == 0:
            return t
        t -= LANE
    return n


def _adaln_body(c_ref, w_ref, b_ref, o_ref):
    cv = c_ref[...]
    o_ref[...] = _dot(cv * jax.nn.sigmoid(cv), w_ref[...]) + b_ref[...]


def _adaln(cv, w_ada, b_ada):
    rows, d = cv.shape
    n = w_ada.shape[1]
    bn = _tile(n, 512)
    return pl.pallas_call(
        _adaln_body,
        grid=(n // bn,),
        in_specs=[pl.BlockSpec((rows, d), lambda j: (0, 0)),
                  pl.BlockSpec((d, bn), lambda j: (0, j)),
                  pl.BlockSpec((1, bn), lambda j: (0, j))],
        out_specs=pl.BlockSpec((rows, bn), lambda j: (0, j)),
        out_shape=jax.ShapeDtypeStruct((rows, n), F32),
        compiler_params=_params(1),
        name="adaln",
    )(cv, w_ada, b_ada.reshape(1, n))


def _norm_mod_body(x_ref, g_ref, sh_ref, sc_ref, o_ref):
    x = x_ref[...]
    y = x * lax.rsqrt(jnp.mean(x * x, axis=-1, keepdims=True) + EPS) * g_ref[...]
    o_ref[...] = (y * (1.0 + sc_ref[...]) + sh_ref[...]).astype(o_ref.dtype)


def _norm_mod(x2, g, shift, scale, rows_per_batch, out_dtype=BF16):
    m, d = x2.shape
    ts = _tile(rows_per_batch, 256)
    tpb = rows_per_batch // ts
    per_batch = shift.shape[0] > 1
    mod_spec = pl.BlockSpec((None, 1, d), (lambda i: (i // tpb, 0, 0)) if per_batch else (lambda i: (0, 0, 0)))
    return pl.pallas_call(
        _norm_mod_body,
        grid=(m // ts,),
        in_specs=[pl.BlockSpec((ts, d), lambda i: (i, 0)),
                  pl.BlockSpec((1, d), lambda i: (0, 0)),
                  mod_spec, mod_spec],
        out_specs=pl.BlockSpec((ts, d), lambda i: (i, 0)),
        out_shape=jax.ShapeDtypeStruct((m, d), out_dtype),
        compiler_params=_params(1),
        name="norm_mod",
    )(x2, g.reshape(1, d), shift, scale)


def _a_spec(bm, k):
    return pl.BlockSpec((bm, k), lambda i, j: (i, 0))


def _w_spec(k, bn):
    return pl.BlockSpec((k, bn), lambda i, j: (0, j))


def _mn_spec(bm, bn):
    return pl.BlockSpec((bm, bn), lambda i, j: (i, j))


def _pos_spec(bm, width, tiles_per_batch):
    return pl.BlockSpec((bm, width), lambda i, j: (i % tiles_per_batch, 0))


def _mm_call(body, grid, in_specs, out_specs, out_shape, args, name):
    return pl.pallas_call(body, grid=grid, in_specs=in_specs, out_specs=out_specs, out_shape=out_shape,
                          compiler_params=_params(2), name=name)(*args)


def _plain_body(a_ref, w_ref, o_ref):
    o_ref[...] = _dot(a_ref[...], w_ref[...]).astype(o_ref.dtype)


def _mm_plain(a, w, bm=1024, bn=512, name="mm_plain"):
    m, k = a.shape
    n = w.shape[1]
    bm, bn = _tile(m, bm), _tile(n, bn)
    return _mm_call(_plain_body, (m // bm, n // bn), [_a_spec(bm, k), _w_spec(k, bn)], _mn_spec(bm, bn),
                    jax.ShapeDtypeStruct((m, n), BF16), (a, w), name)


def _rope_chunks(acc, cos, sin):
    outs = []
    for c in range(acc.shape[1] // LANE):
        t = acc[:, c * LANE:(c + 1) * LANE]
        outs.append(t * cos + pltpu.roll(t, LANE // 2, 1) * sin)
    return outs[0] if len(outs) == 1 else jnp.concatenate(outs, axis=1)


def _rope_body(a_ref, w_ref, cos_ref, sin_ref, o_ref):
    acc = _dot(a_ref[...], w_ref[...])
    o_ref[...] = _rope_chunks(acc, cos_ref[...], sin_ref[...]).astype(o_ref.dtype)


def _mm_rope(a, w, cos2, sin2, rows_per_batch, bm=1024, bn=512):
    m, k = a.shape
    n = w.shape[1]
    bm, bn = _tile(rows_per_batch, bm), _tile(n, bn)
    tpb = rows_per_batch // bm
    return _mm_call(_rope_body, (m // bm, n // bn),
                    [_a_spec(bm, k), _w_spec(k, bn), _pos_spec(bm, LANE, tpb), _pos_spec(bm, LANE, tpb)],
                    _mn_spec(bm, bn), jax.ShapeDtypeStruct((m, n), BF16), (a, w, cos2, sin2), "mm_rope")


def _ckv_body(a_ref, w_ref, g_ref, *rest, rope):
    if rope:
        cos_ref, sin_ref, ckv_ref, kr_ref = rest
    else:
        ckv_ref, kr_ref = rest
    acc = _dot(a_ref[...], w_ref[...])
    nl = ckv_ref.shape[1]
    t = acc[:, :nl]
    ckv_ref[...] = (t * lax.rsqrt(jnp.mean(t * t, axis=-1, keepdims=True) + EPS) * g_ref[...]).astype(ckv_ref.dtype)
    r = acc[:, nl:]
    if rope:
        r = _rope_chunks(r, cos_ref[...], sin_ref[...])
    kr_ref[...] = r.astype(kr_ref.dtype)


def _mm_ckv(a, w, g, cos2, sin2, rows_per_batch, rope, bm=1024):
    m, k = a.shape
    n = w.shape[1]
    nl = n - LANE
    bm = _tile(rows_per_batch, bm)
    tpb = rows_per_batch // bm
    in_specs = [_a_spec(bm, k), _w_spec(k, n), pl.BlockSpec((1, nl), lambda i, j: (0, 0))]
    args = [a, w, g.reshape(1, nl)]
    if rope:
        in_specs += [_pos_spec(bm, LANE, tpb), _pos_spec(bm, LANE, tpb)]
        args += [cos2, sin2]
    return _mm_call(functools.partial(_ckv_body, rope=rope), (m // bm, 1), in_specs,
                    [pl.BlockSpec((bm, nl), lambda i, j: (i, 0)), pl.BlockSpec((bm, LANE), lambda i, j: (i, 0))],
                    [jax.ShapeDtypeStruct((m, nl), BF16), jax.ShapeDtypeStruct((m, LANE), BF16)], args, "mm_ckv")


def _rms_body(a_ref, w_ref, g_ref, o_ref):
    t = _dot(a_ref[...], w_ref[...])
    o_ref[...] = (t * lax.rsqrt(jnp.mean(t * t, axis=-1, keepdims=True) + EPS) * g_ref[...]).astype(o_ref.dtype)


def _mm_rms(a, w, g, bm=1024):
    m, k = a.shape
    n = w.shape[1]
    bm = _tile(m, bm)
    return _mm_call(_rms_body, (m // bm, 1), [_a_spec(bm, k), _w_spec(k, n), pl.BlockSpec((1, n), lambda i, j: (0, 0))],
                    _mn_spec(bm, n), jax.ShapeDtypeStruct((m, n), BF16), (a, w, g.reshape(1, n)), "mm_rms")


def _sigmoid_body(a_ref, w_ref, o_ref):
    o_ref[...] = jax.nn.sigmoid(_dot(a_ref[...], w_ref[...]))


def _mm_sigmoid(a, w, bm=1024, bn=512):
    m, k = a.shape
    n = w.shape[1]
    bm, bn = _tile(m, bm), _tile(n, bn)
    return _mm_call(_sigmoid_body, (m // bm, n // bn), [_a_spec(bm, k), _w_spec(k, bn)], _mn_spec(bm, bn),
                    jax.ShapeDtypeStruct((m, n), F32), (a, w), "mm_gates")


def _qb_body(a_ref, w_ref, *rest, rope):
    if rope:
        cos_ref, sin_ref, o_ref = rest
    else:
        (o_ref,) = rest
    acc = _dot(a_ref[...], w_ref[...])
    if rope:
        acc = jnp.concatenate([acc[:, :NOPE_DIM], _rope_chunks(acc[:, NOPE_DIM:], cos_ref[...], sin_ref[...])], axis=1)
    o_ref[...] = acc.astype(o_ref.dtype)


def _mm_qb(a, w, cos2, sin2, rows_per_batch, rope, bm=1024):
    m, k = a.shape
    n = w.shape[1]
    bm = _tile(rows_per_batch, bm)
    tpb = rows_per_batch // bm
    in_specs = [_a_spec(bm, k), _w_spec(k, MLA_QK)]
    args = [a, w]
    if rope:
        in_specs += [_pos_spec(bm, LANE, tpb), _pos_spec(bm, LANE, tpb)]
        args += [cos2, sin2]
    return _mm_call(functools.partial(_qb_body, rope=rope), (m // bm, n // MLA_QK), in_specs, _mn_spec(bm, MLA_QK),
                    jax.ShapeDtypeStruct((m, n), BF16), args, "mm_qb")


def _kexp_body(a_ref, w_ref, kr_ref, o_ref):
    o_ref[...] = jnp.concatenate([_dot(a_ref[...], w_ref[...]).astype(o_ref.dtype), kr_ref[...]], axis=1)


def _mm_kexp(ckv, w_uk, kr, bm=512):
    m, k = ckv.shape
    n = w_uk.shape[1]
    bm = _tile(m, bm)
    return _mm_call(_kexp_body, (m // bm, n // NOPE_DIM),
                    [_a_spec(bm, k), _w_spec(k, NOPE_DIM), pl.BlockSpec((bm, LANE), lambda i, j: (i, 0))],
                    _mn_spec(bm, MLA_QK), jax.ShapeDtypeStruct((m, (n // NOPE_DIM) * MLA_QK), BF16),
                    (ckv, w_uk, kr), "mm_kexp")


def _merge_body(oa_ref, ob_ref, oc_ref, wa_ref, wb_ref, wc_ref, ga_ref, gb_ref, gc_ref, o_ref):
    y = ga_ref[...] * _dot(oa_ref[...], wa_ref[...])
    y = y + gb_ref[...] * _dot(ob_ref[...], wb_ref[...])
    y = y + gc_ref[...] * _dot(oc_ref[...], wc_ref[...])
    o_ref[...] = y.astype(o_ref.dtype)


def _mm_merge(o_a, o_b, o_c, w_br, gates, bm=512, bn=512):
    m, k = o_a.shape
    n = w_br.shape[2]
    bm, bn = _tile(m, bm), _tile(n, bn)
    nb = n // bn
    wspec = [pl.BlockSpec((None, k, bn), (lambda i, j, r=r: (r, 0, j))) for r in range(N_BRANCH)]
    gspec = [pl.BlockSpec((bm, bn), (lambda i, j, r=r: (i, r * nb + j))) for r in range(N_BRANCH)]
    return _mm_call(_merge_body, (m // bm, nb), [_a_spec(bm, k)] * 3 + wspec + gspec, _mn_spec(bm, bn),
                    jax.ShapeDtypeStruct((m, n), BF16), (o_a, o_b, o_c, w_br, w_br, w_br, gates, gates, gates),
                    "mm_merge")


def _resid_body(a_ref, w_ref, x_ref, gt_ref, o_ref):
    o_ref[...] = x_ref[...] + gt_ref[...] * _dot(a_ref[...], w_ref[...])


def _mm_resid(a, w, x2, gate, rows_per_batch, bm=1024, bn=512, name="mm_resid"):
    m, k = a.shape
    n = w.shape[1]
    bm, bn = _tile(rows_per_batch, bm), _tile(n, bn)
    tpb = rows_per_batch // bm
    per_batch = gate.shape[0] > 1
    gspec = pl.BlockSpec((None, 1, bn), (lambda i, j: (i // tpb, 0, j)) if per_batch else (lambda i, j: (0, 0, j)))
    return _mm_call(_resid_body, (m // bm, n // bn), [_a_spec(bm, k), _w_spec(k, bn), _mn_spec(bm, bn), gspec],
                    _mn_spec(bm, bn), jax.ShapeDtypeStruct((m, n), F32), (a, w, x2, gate), name)


def _ffn_up_body(a_ref, wg_ref, wu_ref, o_ref):
    a = a_ref[...]
    g = _dot(a, wg_ref[...])
    o_ref[...] = (g * jax.nn.sigmoid(g) * _dot(a, wu_ref[...])).astype(o_ref.dtype)


def _mm_ffn_up(a, wg, wu, bm=1024, bn=256):
    m, k = a.shape
    n = wg.shape[1]
    bm, bn = _tile(m, bm), _tile(n, bn)
    return _mm_call(_ffn_up_body, (m // bm, n // bn), [_a_spec(bm, k), _w_spec(k, bn), _w_spec(k, bn)],
                    _mn_spec(bm, bn), jax.ShapeDtypeStruct((m, n), BF16), (a, wg, wu), "mm_ffn_up")


def _win_body(sink_ref, q_ref, kp_ref, kc_ref, kn_ref, vp_ref, vc_ref, vn_ref, kx_ref, vx_ref, o_ref, *, groups):
    kvh = pl.program_id(1)
    n = pl.program_id(2)
    nblk = pl.num_programs(2)
    k_loc = jnp.concatenate([kp_ref[...], kc_ref[...], kn_ref[...]], axis=0)
    v_loc = jnp.concatenate([vp_ref[...], vc_ref[...], vn_ref[...]], axis=0)
    kx = kx_ref[...]
    vx = vx_ref[...]
    qi = lax.broadcasted_iota(jnp.int32, (BLOCK, 3 * BLOCK), 0)
    kj = lax.broadcasted_iota(jnp.int32, (BLOCK, 3 * BLOCK), 1)
    dist = kj - qi - BLOCK
    lo = jnp.where(n > 0, 0, BLOCK)
    hi = jnp.where(n < nblk - 1, 3 * BLOCK, 2 * BLOCK)
    valid = (jnp.abs(dist) <= WINDOW) & (kj >= lo) & (kj < hi)
    scale = HEAD_DIM ** -0.5
    for g in range(groups):
        q = q_ref[:, g * HEAD_DIM:(g + 1) * HEAD_DIM]
        s_loc = jnp.where(valid, _dot_nt(q, k_loc) * scale, NEG_INF)
        s_ctx = _dot_nt(q, kx) * scale
        sink = sink_ref[kvh * groups + g]
        m = jnp.maximum(jnp.maximum(jnp.max(s_loc, axis=1, keepdims=True), jnp.max(s_ctx, axis=1, keepdims=True)), sink)
        p_loc = jnp.exp(s_loc - m)
        p_ctx = jnp.exp(s_ctx - m)
        l = jnp.sum(p_loc, axis=1, keepdims=True) + jnp.sum(p_ctx, axis=1, keepdims=True) + jnp.exp(sink - m)
        o = _dot(p_loc.astype(BF16), v_loc) + _dot(p_ctx.astype(BF16), vx)
        o_ref[:, g * HEAD_DIM:(g + 1) * HEAD_DIM] = (o / l).astype(o_ref.dtype)


def _dense_body(*refs, tk, scale, groups, has_sink):
    if has_sink:
        sink_ref, q_ref, k_ref, v_ref, o_ref = refs
    else:
        q_ref, k_ref, v_ref, o_ref = refs
    q = q_ref[...]
    lk = k_ref.shape[0]
    m = l = acc = None
    for c in range(lk // tk):
        s = _dot_nt(q, k_ref[c * tk:(c + 1) * tk, :]) * scale
        mc = jnp.max(s, axis=1, keepdims=True)
        if c == 0:
            m_new = mc
            if has_sink:
                sink = sink_ref[pl.program_id(1)]
                m_new = jnp.maximum(m_new, sink)
        else:
            m_new = jnp.maximum(m, mc)
        p = jnp.exp(s - m_new)
        pv = _dot(p.astype(BF16), v_ref[c * tk:(c + 1) * tk, :])
        ps = jnp.sum(p, axis=1, keepdims=True)
        if c == 0:
            l = ps + jnp.exp(sink - m_new) if has_sink else ps
            acc = pv
        else:
            alpha = jnp.exp(m - m_new)
            l = alpha * l + ps
            acc = alpha * acc + pv
        m = m_new
    o_ref[...] = (acc / l).astype(o_ref.dtype)


def _dense_attention(q3, k3, v3, *, heads, groups, dk, dv, q_off, k_off, v_off, lk, tq, tk, scale, sink=None,
                     name="dense_attn"):
    bsz, lq, _ = q3.shape
    tq = _tile(lq, tq)
    in_specs = [pl.BlockSpec((None, tq, dk), lambda b, h, i: (b, i, q_off + h)),
                pl.BlockSpec((None, lk, dk), lambda b, h, i: (b, 0, k_off + h // groups)),
                pl.BlockSpec((None, lk, dv), lambda b, h, i: (b, 0, v_off + h // groups))]
    args = [q3, k3, v3]
    if sink is not None:
        in_specs = [pl.BlockSpec(memory_space=pltpu.SMEM)] + in_specs
        args = [sink] + args
    return pl.pallas_call(
        functools.partial(_dense_body, tk=tk, scale=scale, groups=groups, has_sink=sink is not None),
        grid=(bsz, heads, lq // tq),
        in_specs=in_specs,
        out_specs=pl.BlockSpec((None, tq, dv), lambda b, h, i: (b, i, h)),
        out_shape=jax.ShapeDtypeStruct((bsz, lq, heads * dv), BF16),
        compiler_params=_params(3),
        name=name,
    )(*args)


def _na_body(q_ref, kp_ref, kc_ref, kn_ref, vp_ref, vc_ref, vn_ref, kx_ref, vx_ref, bias_ref, o_ref):
    scale = HEAD_DIM ** -0.5
    q = q_ref[...]
    k_loc = jnp.concatenate([kp_ref[...], kc_ref[...], kn_ref[...]], axis=0)
    v_loc = jnp.concatenate([vp_ref[...], vc_ref[...], vn_ref[...]], axis=0)
    bias = bias_ref[...]
    s_loc = jnp.where(bias > 0.5 * NEG_INF, _dot_nt(q, k_loc) * scale + bias, NEG_INF)
    s_ctx = _dot_nt(q, kx_ref[...]) * scale
    m = jnp.maximum(jnp.max(s_loc, axis=1, keepdims=True), jnp.max(s_ctx, axis=1, keepdims=True))
    p_loc = jnp.exp(s_loc - m)
    p_ctx = jnp.exp(s_ctx - m)
    l = jnp.sum(p_loc, axis=1, keepdims=True) + jnp.sum(p_ctx, axis=1, keepdims=True)
    o = _dot(p_loc.astype(BF16), v_loc) + _dot(p_ctx.astype(BF16), vx_ref[...])
    o_ref[...] = (o / l).astype(o_ref.dtype)


def _na_bias_index(rows):
    r = NA_QROWS
    nblk = rows // r
    assert rows % r == 0 and nblk >= 3 and rows >= NA_ROWS
    kwin = min(NA_ROWS, rows)
    idx, valid = [], []
    for i in (0, 1, nblk - 1):
        qrow = (r * i + np.arange(r))[:, None, None, None]
        qcol = np.arange(GRID_W)[None, :, None, None]
        krow = (r * (i - 1) + np.arange(3 * r))[None, None, :, None]
        kcol = np.arange(GRID_W)[None, None, None, :]
        r0 = np.clip(qrow - kwin // 2, 0, rows - kwin)
        c0 = np.clip(qcol - NA_COLS // 2, 0, GRID_W - NA_COLS)
        ok = (krow >= 0) & (krow < rows) & (krow >= r0) & (krow < r0 + kwin) & (kcol >= c0) & (kcol < c0 + NA_COLS)
        roff = np.clip(krow - qrow + NA_ROWS - 1, 0, 2 * NA_ROWS - 2)
        coff = np.clip(kcol - qcol + NA_COLS - 1, 0, 2 * NA_COLS - 2)
        flat = np.broadcast_to(roff * (2 * NA_COLS - 1) + coff, ok.shape)
        idx.append(flat.reshape(r * GRID_W, 3 * r * GRID_W))
        valid.append(np.broadcast_to(ok, ok.shape).reshape(r * GRID_W, 3 * r * GRID_W))
    return np.stack(idx).astype(np.int32), np.stack(valid)


def _neighborhood_attention(plain, plain_c, rpb, bsz, seq, q_off, k_off, v_off):
    rows = seq // GRID_W
    tq = NA_QROWS * GRID_W
    nblk = rows // NA_QROWS
    ctx = plain_c.shape[0] // bsz
    idx, valid = _na_bias_index(rows)
    rpb_flat = rpb.reshape(C_HEADS, -1)
    bias = jnp.where(jnp.asarray(valid)[None], jnp.take(rpb_flat, jnp.asarray(idx), axis=1), NEG_INF)

    def blk(off, d):
        def imap(b, h, i):
            return (b * nblk + jnp.clip(i + d, 0, nblk - 1), off + h)
        return pl.BlockSpec((tq, HEAD_DIM), imap)

    def bias_map(b, h, i):
        return (h, jnp.where(i == 0, 0, jnp.where(i == nblk - 1, 2, 1)), 0, 0)

    in_specs = [blk(q_off, 0),
                blk(k_off, -1), blk(k_off, 0), blk(k_off, 1),
                blk(v_off, -1), blk(v_off, 0), blk(v_off, 1),
                pl.BlockSpec((ctx, HEAD_DIM), lambda b, h, i: (b, k_off + h)),
                pl.BlockSpec((ctx, HEAD_DIM), lambda b, h, i: (b, v_off + h)),
                pl.BlockSpec((None, None, tq, 3 * tq), bias_map)]
    return pl.pallas_call(
        _na_body,
        grid=(bsz, C_HEADS, nblk),
        in_specs=in_specs,
        out_specs=pl.BlockSpec((tq, HEAD_DIM), lambda b, h, i: (b * nblk + i, h)),
        out_shape=jax.ShapeDtypeStruct((bsz * seq, C_HEADS * HEAD_DIM), BF16),
        compiler_params=_params(3),
        name="na_attn",
    )(plain, plain, plain, plain, plain, plain, plain, plain_c, plain_c, bias)


def _rope_tables(seq):
    t = jnp.arange(seq, dtype=jnp.int32)
    row = (t // GRID_W).astype(F32)
    col = (t % GRID_W).astype(F32)

    def cs(rot_dim):
        n_freq = rot_dim // 4
        inv = jnp.power(ROPE_BASE, -jnp.arange(n_freq, dtype=F32) / n_freq)
        ang = jnp.concatenate([row[:, None] * inv, col[:, None] * inv], axis=-1)
        return jnp.cos(ang), jnp.sin(ang)

    ca, sa = cs(HEAD_DIM)
    cb, sb = cs(ROPE_DIM)
    zb = jnp.zeros_like(cb)
    return (jnp.concatenate([ca, ca], axis=1), jnp.concatenate([-sa, sa], axis=1),
            jnp.concatenate([cb, zb, cb, zb], axis=1), jnp.concatenate([-sb, zb, sb, zb], axis=1))


def _spread_rope_cols(w):
    h = ROPE_DIM // 2
    z = jnp.zeros(w.shape[:-1] + (LANE // 2 - h,), w.dtype)
    return jnp.concatenate([w[..., :h], z, w[..., h:], z], axis=-1)


def _layer_weights(w_in, w_q_b, w_kv_b):
    kvh = A_KV_HEADS * HEAD_DIM
    ch = C_HEADS * HEAD_DIM
    ah = A_HEADS * HEAD_DIM
    sizes = [kvh, kvh, KV_LORA, ROPE_DIM, ch, ch, ah, Q_LORA, ch, N_BRANCH * D_MODEL]
    offs = np.concatenate([[0], np.cumsum(sizes)])
    a_k, a_v, b_ckv, b_kr, c_k, c_v, a_q, b_qa, c_q, gate = [w_in[:, offs[i]:offs[i + 1]] for i in range(len(sizes))]
    w = {
        "rope": jnp.concatenate([a_k, a_q], axis=1).astype(BF16),
        "plain": jnp.concatenate([a_v, c_k, c_v, c_q], axis=1).astype(BF16),
        "ckv": jnp.concatenate([b_ckv, _spread_rope_cols(b_kr)], axis=1).astype(BF16),
        "bqa": b_qa.astype(BF16),
        "gate": gate.astype(BF16),
        "qb": jnp.concatenate([w_q_b[..., :NOPE_DIM], _spread_rope_cols(w_q_b[..., NOPE_DIM:])], axis=-1)
        .reshape(Q_LORA, B_HEADS * MLA_QK).astype(BF16),
        "uk": w_kv_b[..., :NOPE_DIM].reshape(KV_LORA, B_HEADS * NOPE_DIM).astype(BF16),
        "uv": w_kv_b[..., NOPE_DIM:].reshape(KV_LORA, B_HEADS * V_DIM).astype(BF16),
    }
    return w


def _layer(x2, ctx2, cv, tabs, w_ada, b_ada, g_mix, w_in, sink_a, g_q_a, w_q_b, g_kv_a, w_kv_b, rpb_c, w_br,
           w_out, g_ffn, w_ffn_gate, w_ffn_up, w_ffn_down, last):
    bsz, seq, ctx, d = BATCH, SEQ, CTX_LEN, D_MODEL
    cos_a, sin_a, cos_b, sin_b = tabs
    kvb = A_KV_HEADS * HEAD_DIM // LANE
    chb = C_HEADS * HEAD_DIM // LANE
    mla_scale = (NOPE_DIM + ROPE_DIM) ** -0.5

    mod = _adaln(cv, w_ada, b_ada)
    sh_m, sc_m, gt_m, sh_f, sc_f, gt_f = [mod[:bsz, i * d:(i + 1) * d].reshape(bsz, 1, d) for i in range(6)]
    csh_m, csc_m, cgt_m, csh_f, csc_f, cgt_f = [mod[bsz:bsz + 1, i * d:(i + 1) * d].reshape(1, 1, d) for i in range(6)]
    w = _layer_weights(w_in, w_q_b, w_kv_b)
    w_br_b = w_br.astype(BF16)

    hc = _norm_mod(ctx2, g_mix, csh_m, csc_m, ctx)
    ropecols_c = _mm_plain(hc, w["rope"] if not last else w["rope"][:, :kvb * LANE], bm=256, name="mm_plain_c")
    plain_c = _mm_plain(hc, w["plain"] if not last else w["plain"][:, :(kvb + 2 * chb) * LANE], bm=256,
                        name="mm_plain_c")
    ckv_c, kr_c = _mm_ckv(hc, w["ckv"], g_kv_a, None, None, ctx, rope=False, bm=256)

    hx = _norm_mod(x2, g_mix, sh_m, sc_m, seq)
    qk_a = _mm_rope(hx, w["rope"], cos_a, sin_a, seq)
    plain = _mm_plain(hx, w["plain"])
    ckv, kr = _mm_ckv(hx, w["ckv"], g_kv_a, cos_b, sin_b, seq, rope=True)
    bqa = _mm_rms(hx, w["bqa"], g_q_a)
    gates = _mm_sigmoid(hx, w["gate"])
    qb = _mm_qb(bqa, w["qb"], cos_b, sin_b, seq, rope=True)

    lk = ctx + seq
    ckv_all = jnp.concatenate([ckv_c.reshape(bsz, ctx, -1), ckv.reshape(bsz, seq, -1)], axis=1).reshape(bsz * lk, -1)
    kr_all = jnp.concatenate([kr_c.reshape(bsz, ctx, -1), kr.reshape(bsz, seq, -1)], axis=1).reshape(bsz * lk, -1)
    k_exp = _mm_kexp(ckv_all, w["uk"], kr_all).reshape(bsz, lk, -1)
    v_exp = _mm_plain(ckv_all, w["uv"], bm=512, bn=512, name="mm_vexp").reshape(bsz, lk, -1)

    o_a = _window_attention(sink_a, qk_a, plain, ropecols_c, plain_c, bsz, seq, kvb)
    o_b = _dense_attention(qb.reshape(bsz, seq, -1), k_exp, v_exp, heads=B_HEADS, groups=1, dk=MLA_QK, dv=V_DIM,
                           q_off=0, k_off=0, v_off=0, lk=lk, tq=512, tk=_mla_chunk(lk), scale=mla_scale,
                           name="mla_attn").reshape(bsz * seq, -1)
    o_c = _neighborhood_attention(plain, plain_c, rpb_c, bsz, seq, q_off=kvb + 2 * chb, k_off=kvb, v_off=kvb + chb)

    y = _mm_merge(o_a, o_b, o_c, w_br_b, gates)
    x2 = _mm_resid(y, w_out.astype(BF16), x2, gt_m, seq)
    h2 = _norm_mod(x2, g_ffn, sh_f, sc_f, seq)
    act = _mm_ffn_up(h2, w_ffn_gate.astype(BF16), w_ffn_up.astype(BF16))
    w_down = w_ffn_down.astype(BF16)
    x2 = _mm_resid(act, w_down, x2, gt_f, seq, bm=512, bn=256, name="mm_ffn_down")

    if not last:
        gates_c = _mm_sigmoid(hc, w["gate"], bm=256)
        bqa_c = _mm_rms(hc, w["bqa"], g_q_a, bm=256)
        qb_c = _mm_qb(bqa_c, w["qb"], None, None, ctx, rope=False, bm=256)
        rc3 = ropecols_c.reshape(bsz, ctx, -1)
        pc3 = plain_c.reshape(bsz, ctx, -1)
        oc_a = _dense_attention(rc3, rc3, pc3, heads=A_HEADS, groups=A_HEADS // A_KV_HEADS, dk=HEAD_DIM, dv=HEAD_DIM,
                                q_off=kvb, k_off=0, v_off=0, lk=ctx, tq=ctx, tk=ctx, scale=HEAD_DIM ** -0.5,
                                sink=sink_a, name="ctx_attn_a").reshape(bsz * ctx, -1)
        oc_b = _dense_attention(qb_c.reshape(bsz, ctx, -1), k_exp, v_exp, heads=B_HEADS, groups=1, dk=MLA_QK,
                                dv=V_DIM, q_off=0, k_off=0, v_off=0, lk=ctx, tq=ctx, tk=ctx, scale=mla_scale,
                                name="ctx_attn_b").reshape(bsz * ctx, -1)
        oc_c = _dense_attention(pc3, pc3, pc3, heads=C_HEADS, groups=1, dk=HEAD_DIM, dv=HEAD_DIM,
                                q_off=kvb + 2 * chb, k_off=kvb, v_off=kvb + chb, lk=ctx, tq=ctx, tk=ctx,
                                scale=HEAD_DIM ** -0.5, name="ctx_attn_c").reshape(bsz * ctx, -1)
        yc = _mm_merge(oc_a, oc_b, oc_c, w_br_b, gates_c, bm=256)
        ctx2 = _mm_resid(yc, w_out.astype(BF16), ctx2, cgt_m, ctx, bm=256, name="mm_resid_c")
        hc2 = _norm_mod(ctx2, g_ffn, csh_f, csc_f, ctx)
        act_c = _mm_ffn_up(hc2, w_ffn_gate.astype(BF16), w_ffn_up.astype(BF16), bm=256)
        ctx2 = _mm_resid(act_c, w_down, ctx2, cgt_f, ctx, bm=256, bn=256, name="mm_ffn_down_c")
    return x2, ctx2


def _mla_chunk(lk):
    for parts in (3, 4, 2, 1):
        if lk % parts == 0 and (lk // parts) % LANE == 0:
            return lk // parts
    return lk


def _window_attention(sink, qk_a, plain, ropecols_c, plain_c, bsz, seq, kvb):
    nblk = seq // BLOCK
    groups = A_HEADS // A_KV_HEADS
    gw = groups * HEAD_DIM
    ctx = plain_c.shape[0] // bsz

    def blk(d):
        def imap(b, h, n):
            return (b * nblk + jnp.clip(n + d, 0, nblk - 1), h)
        return pl.BlockSpec((BLOCK, HEAD_DIM), imap)

    cspec = pl.BlockSpec((ctx, HEAD_DIM), lambda b, h, n: (b, h))
    in_specs = [pl.BlockSpec(memory_space=pltpu.SMEM),
                pl.BlockSpec((BLOCK, gw), lambda b, h, n: (b * nblk + n, kvb // groups + h)),
                blk(-1), blk(0), blk(1), blk(-1), blk(0), blk(1), cspec, cspec]
    return pl.pallas_call(
        functools.partial(_win_body, groups=groups),
        grid=(bsz, A_KV_HEADS, nblk),
        in_specs=in_specs,
        out_specs=pl.BlockSpec((BLOCK, gw), lambda b, h, n: (b * nblk + n, h)),
        out_shape=jax.ShapeDtypeStruct((bsz * seq, A_HEADS * HEAD_DIM), BF16),
        compiler_params=_params(3),
        name="win_attn",
    )(sink, qk_a, qk_a, qk_a, qk_a, plain, plain, plain, ropecols_c, plain_c)


def kernel(x, c, ctx, c_ctx, w_ada, b_ada, g_mix, w_in, sink_a, g_q_a, w_q_b, g_kv_a, w_kv_b, rpb_c, w_br, w_out,
           g_ffn, w_ffn_gate, w_ffn_up, w_ffn_down, g_final):
    bsz, seq, d = x.shape
    tabs = _rope_tables(seq)
    pad = (-(bsz + 1)) % 8
    cv = jnp.concatenate([c, c_ctx[None, :], jnp.zeros((pad, d), F32)], axis=0)
    x2 = x.reshape(bsz * seq, d)
    ctx2 = ctx.reshape(bsz * ctx.shape[1], d)
    for l in range(DEPTH):
        x2, ctx2 = _layer(x2, ctx2, cv, tabs, w_ada[l], b_ada[l], g_mix[l], w_in[l], sink_a[l], g_q_a[l], w_q_b[l],
                          g_kv_a[l], w_kv_b[l], rpb_c[l], w_br[l], w_out[l], g_ffn[l], w_ffn_gate[l], w_ffn_up[l],
                          w_ffn_down[l], last=(l == DEPTH - 1))
    zero = jnp.zeros((1, 1, d), F32)
    out = _norm_mod(x2, g_final, zero, zero, seq, out_dtype=F32)
    return out.reshape(bsz, seq, d)
```

```python
import functools
import math

import numpy as np
import jax
import jax.numpy as jnp
from jax import lax
from jax.experimental import pallas as pl
from jax.experimental.pallas import tpu as pltpu

D_MODEL = 4096
BATCH = 2
SEQ = 8192
DEPTH = 2
GRID_W = 64
CTX_LEN = 256
HEAD_DIM = 128
A_HEADS = 16
A_KV_HEADS = 4
WINDOW = 128
BLOCK = 128
B_HEADS = 16
Q_LORA = 1024
KV_LORA = 512
NOPE_DIM = 128
ROPE_DIM = 64
V_DIM = 128
C_HEADS = 16
NA_ROWS = 8
NA_COLS = 16
N_BRANCH = 3
ROPE_BASE = 10000.0
EPS = 1e-6
NEG_INF = -1e30

LANE = 128
MLA_QK = 2 * LANE
MLA_LOGIT_SCALE = (NOPE_DIM + ROPE_DIM) ** -0.5 * math.log2(math.e)
NA_QROWS = 4
VMEM_LIMIT = 56 * 1024 * 1024

BF16 = jnp.bfloat16
F32 = jnp.float32


def _params(n_axes):
    return pltpu.CompilerParams(dimension_semantics=("parallel",) * n_axes,
                                vmem_limit_bytes=VMEM_LIMIT)


def _dot(a, b):
    return jnp.dot(a, b, preferred_element_type=F32)


def _dot_nt(a, b):
    return lax.dot_general(a, b, (((1,), (1,)), ((), ())), preferred_element_type=F32)


def _tile(n, pref):
    if n <= pref:
        return n
    t = pref - pref % LANE
    while t >= LANE:
        if n % t == 0:
            return t
        t -= LANE
    return n


def _adaln_body(c_ref, w_ref, b_ref, o_ref):
    cv = c_ref[...]
    o_ref[...] = _dot(cv * jax.nn.sigmoid(cv), w_ref[...]) + b_ref[...]


def _adaln(cv, w_ada, b_ada):
    rows, d = cv.shape
    n = w_ada.shape[1]
    bn = _tile(n, 512)
    return pl.pallas_call(
        _adaln_body,
        grid=(n // bn,),
        in_specs=[pl.BlockSpec((rows, d), lambda j: (0, 0)),
                  pl.BlockSpec((d, bn), lambda j: (0, j)),
                  pl.BlockSpec((1, bn), lambda j: (0, j))],
        out_specs=pl.BlockSpec((rows, bn), lambda j: (0, j)),
        out_shape=jax.ShapeDtypeStruct((rows, n), F32),
        compiler_params=_params(1),
        name="adaln",
    )(cv, w_ada, b_ada.reshape(1, n))


def _norm_mod_body(x_ref, g_ref, sh_ref, sc_ref, o_ref):
    x = x_ref[...]
    y = x * lax.rsqrt(jnp.mean(x * x, axis=-1, keepdims=True) + EPS) * g_ref[...]
    o_ref[...] = (y * (1.0 + sc_ref[...]) + sh_ref[...]).astype(o_ref.dtype)


def _norm_mod(x2, g, shift, scale, rows_per_batch, out_dtype=BF16):
    m, d = x2.shape
    ts = _tile(rows_per_batch, 256)
    tpb = rows_per_batch // ts
    per_batch = shift.shape[0] > 1
    mod_spec = pl.BlockSpec((None, 1, d), (lambda i: (i // tpb, 0, 0)) if per_batch else (lambda i: (0, 0, 0)))
    return pl.pallas_call(
        _norm_mod_body,
        grid=(m // ts,),
        in_specs=[pl.BlockSpec((ts, d), lambda i: (i, 0)),
                  pl.BlockSpec((1, d), lambda i: (0, 0)),
                  mod_spec, mod_spec],
        out_specs=pl.BlockSpec((ts, d), lambda i: (i, 0)),
        out_shape=jax.ShapeDtypeStruct((m, d), out_dtype),
        compiler_params=_params(1),
        name="norm_mod",
    )(x2, g.reshape(1, d), shift, scale)


def _a_spec(bm, k):
    return pl.BlockSpec((bm, k), lambda i, j: (i, 0))


def _w_spec(k, bn):
    return pl.BlockSpec((k, bn), lambda i, j: (0, j))


def _mn_spec(bm, bn):
    return pl.BlockSpec((bm, bn), lambda i, j: (i, j))


def _pos_spec(bm, width, tiles_per_batch):
    return pl.BlockSpec((bm, width), lambda i, j: (i % tiles_per_batch, 0))


def _mm_call(body, grid, in_specs, out_specs, out_shape, args, name):
    return pl.pallas_call(body, grid=grid, in_specs=in_specs, out_specs=out_specs, out_shape=out_shape,
                          compiler_params=_params(2), name=name)(*args)


def _plain_body(a_ref, w_ref, o_ref):
    o_ref[...] = _dot(a_ref[...], w_ref[...]).astype(o_ref.dtype)


def _mm_plain(a, w, bm=1024, bn=512, name="mm_plain"):
    m, k = a.shape
    n = w.shape[1]
    bm, bn = _tile(m, bm), _tile(n, bn)
    return _mm_call(_plain_body, (m // bm, n // bn), [_a_spec(bm, k), _w_spec(k, bn)], _mn_spec(bm, bn),
                    jax.ShapeDtypeStruct((m, n), BF16), (a, w), name)


def _rope_chunks(acc, cos, sin):
    outs = []
    for c in range(acc.shape[1] // LANE):
        t = acc[:, c * LANE:(c + 1) * LANE]
        outs.append(t * cos + pltpu.roll(t, LANE // 2, 1) * sin)
    return outs[0] if len(outs) == 1 else jnp.concatenate(outs, axis=1)


def _rope_body(a_ref, w_ref, cos_ref, sin_ref, o_ref):
    acc = _dot(a_ref[...], w_ref[...])
    o_ref[...] = _rope_chunks(acc, cos_ref[...], sin_ref[...]).astype(o_ref.dtype)


def _mm_rope(a, w, cos2, sin2, rows_per_batch, bm=1024, bn=512):
    m, k = a.shape
    n = w.shape[1]
    bm, bn = _tile(rows_per_batch, bm), _tile(n, bn)
    tpb = rows_per_batch // bm
    return _mm_call(_rope_body, (m // bm, n // bn),
                    [_a_spec(bm, k), _w_spec(k, bn), _pos_spec(bm, LANE, tpb), _pos_spec(bm, LANE, tpb)],
                    _mn_spec(bm, bn), jax.ShapeDtypeStruct((m, n), BF16), (a, w, cos2, sin2), "mm_rope")


def _ckv_body(a_ref, w_ref, g_ref, *rest, rope):
    if rope:
        cos_ref, sin_ref, ckv_ref, kr_ref = rest
    else:
        ckv_ref, kr_ref = rest
    acc = _dot(a_ref[...], w_ref[...])
    nl = ckv_ref.shape[1]
    t = acc[:, :nl]
    ckv_ref[...] = (t * lax.rsqrt(jnp.mean(t * t, axis=-1, keepdims=True) + EPS) * g_ref[...]).astype(ckv_ref.dtype)
    r = acc[:, nl:]
    if rope:
        r = _rope_chunks(r, cos_ref[...], sin_ref[...])
    kr_ref[...] = r.astype(kr_ref.dtype)


def _mm_ckv(a, w, g, cos2, sin2, rows_per_batch, rope, bm=1024):
    m, k = a.shape
    n = w.shape[1]
    nl = n - LANE
    bm = _tile(rows_per_batch, bm)
    tpb = rows_per_batch // bm
    in_specs = [_a_spec(bm, k), _w_spec(k, n), pl.BlockSpec((1, nl), lambda i, j: (0, 0))]
    args = [a, w, g.reshape(1, nl)]
    if rope:
        in_specs += [_pos_spec(bm, LANE, tpb), _pos_spec(bm, LANE, tpb)]
        args += [cos2, sin2]
    return _mm_call(functools.partial(_ckv_body, rope=rope), (m // bm, 1), in_specs,
                    [pl.BlockSpec((bm, nl), lambda i, j: (i, 0)), pl.BlockSpec((bm, LANE), lambda i, j: (i, 0))],
                    [jax.ShapeDtypeStruct((m, nl), BF16), jax.ShapeDtypeStruct((m, LANE), BF16)], args, "mm_ckv")


def _rms_body(a_ref, w_ref, g_ref, o_ref):
    t = _dot(a_ref[...], w_ref[...])
    o_ref[...] = (t * lax.rsqrt(jnp.mean(t * t, axis=-1, keepdims=True) + EPS) * g_ref[...]).astype(o_ref.dtype)


def _mm_rms(a, w, g, bm=1024):
    m, k = a.shape
    n = w.shape[1]
    bm = _tile(m, bm)
    return _mm_call(_rms_body, (m // bm, 1), [_a_spec(bm, k), _w_spec(k, n), pl.BlockSpec((1, n), lambda i, j: (0, 0))],
                    _mn_spec(bm, n), jax.ShapeDtypeStruct((m, n), BF16), (a, w, g.reshape(1, n)), "mm_rms")


def _sigmoid_body(a_ref, w_ref, o_ref):
    o_ref[...] = jax.nn.sigmoid(_dot(a_ref[...], w_ref[...]))


def _mm_sigmoid(a, w, bm=1024, bn=512):
    m, k = a.shape
    n = w.shape[1]
    bm, bn = _tile(m, bm), _tile(n, bn)
    return _mm_call(_sigmoid_body, (m // bm, n // bn), [_a_spec(bm, k), _w_spec(k, bn)], _mn_spec(bm, bn),
                    jax.ShapeDtypeStruct((m, n), F32), (a, w), "mm_gates")


def _qb_body(a_ref, w_ref, *rest, rope):
    if rope:
        cos_ref, sin_ref, o_ref = rest
    else:
        (o_ref,) = rest
    acc = _dot(a_ref[...], w_ref[...]) * MLA_LOGIT_SCALE
    if rope:
        parts = []
        for h in range(acc.shape[1] // MLA_QK):
            parts.append(acc[:, h * MLA_QK:h * MLA_QK + NOPE_DIM])
            parts.append(_rope_chunks(acc[:, h * MLA_QK + NOPE_DIM:(h + 1) * MLA_QK], cos_ref[...], sin_ref[...]))
        acc = jnp.concatenate(parts, axis=1)
    o_ref[...] = acc.astype(o_ref.dtype)


def _mm_qb(a, w, cos2, sin2, rows_per_batch, rope, bm=1024):
    m, k = a.shape
    n = w.shape[1]
    bm = _tile(rows_per_batch, bm)
    bn = 2 * MLA_QK if n % (2 * MLA_QK) == 0 else MLA_QK
    tpb = rows_per_batch // bm
    in_specs = [_a_spec(bm, k), _w_spec(k, bn)]
    args = [a, w]
    if rope:
        in_specs += [_pos_spec(bm, LANE, tpb), _pos_spec(bm, LANE, tpb)]
        args += [cos2, sin2]
    return _mm_call(functools.partial(_qb_body, rope=rope), (m // bm, n // bn), in_specs, _mn_spec(bm, bn),
                    jax.ShapeDtypeStruct((m, n), BF16), args, "mm_qb")


def _kexp_body(a_ref, w_ref, kr_ref, o_ref):
    acc = _dot(a_ref[...], w_ref[...]).astype(o_ref.dtype)
    kr = kr_ref[...]
    parts = []
    for h in range(acc.shape[1] // NOPE_DIM):
        parts += [acc[:, h * NOPE_DIM:(h + 1) * NOPE_DIM], kr]
    o_ref[...] = jnp.concatenate(parts, axis=1)


def _mm_kexp(ckv, w_uk, kr, bm=512, heads_per_tile=4):
    m, k = ckv.shape
    n = w_uk.shape[1]
    heads = n // NOPE_DIM
    bm = _tile(m, bm)
    hpt = heads_per_tile if heads % heads_per_tile == 0 else 1
    return _mm_call(_kexp_body, (m // bm, heads // hpt),
                    [_a_spec(bm, k), _w_spec(k, hpt * NOPE_DIM), pl.BlockSpec((bm, LANE), lambda i, j: (i, 0))],
                    _mn_spec(bm, hpt * MLA_QK), jax.ShapeDtypeStruct((m, heads * MLA_QK), BF16),
                    (ckv, w_uk, kr), "mm_kexp")


def _vexp_body(a_ref, w_ref, o_ref):
    acc = _dot(a_ref[...], w_ref[...]).astype(o_ref.dtype)
    lane = lax.broadcasted_iota(jnp.int32, (acc.shape[0], V_DIM), 1)
    ones = jnp.where(lane == 0, 1.0, 0.0).astype(o_ref.dtype)
    parts = []
    for h in range(acc.shape[1] // V_DIM):
        parts += [acc[:, h * V_DIM:(h + 1) * V_DIM], ones]
    o_ref[...] = jnp.concatenate(parts, axis=1)


def _mm_vexp(ckv, w_uv, bm=512, heads_per_tile=4):
    m, k = ckv.shape
    heads = w_uv.shape[1] // V_DIM
    bm = _tile(m, bm)
    hpt = heads_per_tile if heads % heads_per_tile == 0 else 1
    return _mm_call(_vexp_body, (m // bm, heads // hpt), [_a_spec(bm, k), _w_spec(k, hpt * V_DIM)],
                    _mn_spec(bm, 2 * hpt * V_DIM), jax.ShapeDtypeStruct((m, 2 * heads * V_DIM), BF16),
                    (ckv, w_uv), "mm_vexp")


def _merge_body(oa_ref, ob_ref, oc_ref, wa_ref, wb_ref, wc_ref, ga_ref, gb_ref, gc_ref, o_ref):
    y = ga_ref[...] * _dot(oa_ref[...], wa_ref[...])
    y = y + gb_ref[...] * _dot(ob_ref[...], wb_ref[...])
    y = y + gc_ref[...] * _dot(oc_ref[...], wc_ref[...])
    o_ref[...] = y.astype(o_ref.dtype)


def _mm_merge(o_a, o_b, o_c, w_br, gates, bm=512, bn=512):
    m, k = o_a.shape
    n = w_br.shape[2]
    bm, bn = _tile(m, bm), _tile(n, bn)
    nb = n // bn
    wspec = [pl.BlockSpec((None, k, bn), (lambda i, j, r=r: (r, 0, j))) for r in range(N_BRANCH)]
    gspec = [pl.BlockSpec((bm, bn), (lambda i, j, r=r: (i, r * nb + j))) for r in range(N_BRANCH)]
    return _mm_call(_merge_body, (m // bm, nb), [_a_spec(bm, k)] * 3 + wspec + gspec, _mn_spec(bm, bn),
                    jax.ShapeDtypeStruct((m, n), BF16), (o_a, o_b, o_c, w_br, w_br, w_br, gates, gates, gates),
                    "mm_merge")


def _resid_body(a_ref, w_ref, x_ref, gt_ref, o_ref):
    o_ref[...] = x_ref[...] + gt_ref[...] * _dot(a_ref[...], w_ref[...])


def _mm_resid(a, w, x2, gate, rows_per_batch, bm=1024, bn=512, name="mm_resid"):
    m, k = a.shape
    n = w.shape[1]
    bm, bn = _tile(rows_per_batch, bm), _tile(n, bn)
    tpb = rows_per_batch // bm
    per_batch = gate.shape[0] > 1
    gspec = pl.BlockSpec((None, 1, bn), (lambda i, j: (i // tpb, 0, j)) if per_batch else (lambda i, j: (0, 0, j)))
    return _mm_call(_resid_body, (m // bm, n // bn), [_a_spec(bm, k), _w_spec(k, bn), _mn_spec(bm, bn), gspec],
                    _mn_spec(bm, bn), jax.ShapeDtypeStruct((m, n), F32), (a, w, x2, gate), name)


def _ffn_up_body(a_ref, wg_ref, wu_ref, o_ref):
    a = a_ref[...]
    g = _dot(a, wg_ref[...])
    o_ref[...] = (g * jax.nn.sigmoid(g) * _dot(a, wu_ref[...])).astype(o_ref.dtype)


def _mm_ffn_up(a, wg, wu, bm=1024, bn=256):
    m, k = a.shape
    n = wg.shape[1]
    bm, bn = _tile(m, bm), _tile(n, bn)
    return _mm_call(_ffn_up_body, (m // bm, n // bn), [_a_spec(bm, k), _w_spec(k, bn), _w_spec(k, bn)],
                    _mn_spec(bm, bn), jax.ShapeDtypeStruct((m, n), BF16), (a, wg, wu), "mm_ffn_up")


def _win_body(sink_ref, q_ref, kp_ref, kc_ref, kn_ref, vp_ref, vc_ref, vn_ref, kx_ref, vx_ref, o_ref, *, groups):
    kvh = pl.program_id(1)
    n = pl.program_id(2)
    nblk = pl.num_programs(2)
    k_loc = jnp.concatenate([kp_ref[...], kc_ref[...], kn_ref[...]], axis=0)
    v_loc = jnp.concatenate([vp_ref[...], vc_ref[...], vn_ref[...]], axis=0)
    kx = kx_ref[...]
    vx = vx_ref[...]
    qi = lax.broadcasted_iota(jnp.int32, (BLOCK, 3 * BLOCK), 0)
    kj = lax.broadcasted_iota(jnp.int32, (BLOCK, 3 * BLOCK), 1)
    dist = kj - qi - BLOCK
    lo = jnp.where(n > 0, 0, BLOCK)
    hi = jnp.where(n < nblk - 1, 3 * BLOCK, 2 * BLOCK)
    valid = (jnp.abs(dist) <= WINDOW) & (kj >= lo) & (kj < hi)
    scale = HEAD_DIM ** -0.5
    for g in range(groups):
        q = q_ref[:, g * HEAD_DIM:(g + 1) * HEAD_DIM]
        s_loc = jnp.where(valid, _dot_nt(q, k_loc) * scale, NEG_INF)
        s_ctx = _dot_nt(q, kx) * scale
        sink = sink_ref[kvh * groups + g]
        m = jnp.maximum(jnp.maximum(jnp.max(s_loc, axis=1, keepdims=True), jnp.max(s_ctx, axis=1, keepdims=True)), sink)
        p_loc = jnp.exp(s_loc - m)
        p_ctx = jnp.exp(s_ctx - m)
        l = jnp.sum(p_loc, axis=1, keepdims=True) + jnp.sum(p_ctx, axis=1, keepdims=True) + jnp.exp(sink - m)
        o = _dot(p_loc.astype(BF16), v_loc) + _dot(p_ctx.astype(BF16), vx)
        o_ref[:, g * HEAD_DIM:(g + 1) * HEAD_DIM] = (o / l).astype(o_ref.dtype)


def _dense_body(*refs, tk, scale, has_sink, ones_col):
    if has_sink:
        sink_ref, q_ref, k_ref, v_ref, o_ref = refs
    else:
        q_ref, k_ref, v_ref, o_ref = refs
    q = q_ref[...]
    lk = k_ref.shape[0]
    dv = o_ref.shape[1]
    exp = jnp.exp2 if scale is None else jnp.exp
    m = l = acc = None
    for c in range(lk // tk):
        s = _dot_nt(q, k_ref[c * tk:(c + 1) * tk, :])
        if scale is not None:
            s = s * scale
        mc = jnp.max(s, axis=1, keepdims=True)
        if c == 0:
            m_new = mc
            if has_sink:
                sink = sink_ref[pl.program_id(1)]
                m_new = jnp.maximum(m_new, sink)
        else:
            m_new = jnp.maximum(m, mc)
        p = exp(s - m_new)
        pv = _dot(p.astype(BF16), v_ref[c * tk:(c + 1) * tk, :])
        if not ones_col:
            ps = jnp.sum(p, axis=1, keepdims=True)
            if c == 0:
                l = ps + exp(sink - m_new) if has_sink else ps
            else:
                l = exp(m - m_new) * l + ps
        acc = pv if c == 0 else exp(m - m_new) * acc + pv
        m = m_new
    if ones_col:
        l = acc[:, dv:dv + 1]
        acc = acc[:, :dv]
    o_ref[...] = (acc / l).astype(o_ref.dtype)


def _dense_attention(q3, k3, v3, *, heads, groups, dk, dv, q_off, k_off, v_off, lk, tq, tk, scale, sink=None,
                     ones_col=False, name="dense_attn"):
    assert not (ones_col and sink is not None)
    bsz, lq, _ = q3.shape
    tq = _tile(lq, tq)
    dvw = 2 * dv if ones_col else dv
    in_specs = [pl.BlockSpec((None, tq, dk), lambda b, h, i: (b, i, q_off + h)),
                pl.BlockSpec((None, lk, dk), lambda b, h, i: (b, 0, k_off + h // groups)),
                pl.BlockSpec((None, lk, dvw), lambda b, h, i: (b, 0, v_off + h // groups))]
    args = [q3, k3, v3]
    if sink is not None:
        in_specs = [pl.BlockSpec(memory_space=pltpu.SMEM)] + in_specs
        args = [sink] + args
    return pl.pallas_call(
        functools.partial(_dense_body, tk=tk, scale=scale, has_sink=sink is not None, ones_col=ones_col),
        grid=(bsz, heads, lq // tq),
        in_specs=in_specs,
        out_specs=pl.BlockSpec((None, tq, dv), lambda b, h, i: (b, i, h)),
        out_shape=jax.ShapeDtypeStruct((bsz, lq, heads * dv), BF16),
        compiler_params=_params(3),
        name=name,
    )(*args)


def _na_body(q_ref, kp_ref, kc_ref, kn_ref, vp_ref, vc_ref, vn_ref, kx_ref, vx_ref, bias_ref, o_ref):
    scale = HEAD_DIM ** -0.5
    for h in range(bias_ref.shape[0]):
        cols = slice(h * HEAD_DIM, (h + 1) * HEAD_DIM)
        q = q_ref[:, cols]
        k_loc = jnp.concatenate([kp_ref[:, cols], kc_ref[:, cols], kn_ref[:, cols]], axis=0)
        v_loc = jnp.concatenate([vp_ref[:, cols], vc_ref[:, cols], vn_ref[:, cols]], axis=0)
        bias = bias_ref[h]
        s_loc = jnp.where(bias > 0.5 * NEG_INF, _dot_nt(q, k_loc) * scale + bias, NEG_INF)
        s_ctx = _dot_nt(q, kx_ref[:, cols]) * scale
        m = jnp.maximum(jnp.max(s_loc, axis=1, keepdims=True), jnp.max(s_ctx, axis=1, keepdims=True))
        p_loc = jnp.exp(s_loc - m)
        p_ctx = jnp.exp(s_ctx - m)
        l = jnp.sum(p_loc, axis=1, keepdims=True) + jnp.sum(p_ctx, axis=1, keepdims=True)
        o = _dot(p_loc.astype(BF16), v_loc) + _dot(p_ctx.astype(BF16), vx_ref[:, cols])
        o_ref[:, cols] = (o / l).astype(o_ref.dtype)


def _na_valid(rows):
    r = NA_QROWS
    nblk = rows // r
    assert rows % r == 0 and nblk >= 3 and rows >= NA_ROWS and NA_ROWS >= 2 * r
    kwin = min(NA_ROWS, rows)
    valid = []
    for i in (0, 1, nblk - 1):
        qrow = (r * i + np.arange(r))[:, None, None, None]
        qcol = np.arange(GRID_W)[None, :, None, None]
        krow = (r * (i - 1) + np.arange(3 * r))[None, None, :, None]
        kcol = np.arange(GRID_W)[None, None, None, :]
        r0 = np.clip(qrow - kwin // 2, 0, rows - kwin)
        c0 = np.clip(qcol - NA_COLS // 2, 0, GRID_W - NA_COLS)
        ok = (krow >= 0) & (krow < rows) & (krow >= r0) & (krow < r0 + kwin) & (kcol >= c0) & (kcol < c0 + NA_COLS)
        valid.append(ok.reshape(r * GRID_W, 3 * r * GRID_W))
    return np.stack(valid)


def _na_bias_tables(rpb, rows):
    r, w = NA_QROWS, GRID_W
    heads, nro, nco = rpb.shape
    circ = jnp.concatenate([rpb[..., NA_COLS - 1:], jnp.zeros((heads, nro, 2 * w - nco), rpb.dtype),
                            rpb[..., :NA_COLS - 1]], axis=-1)
    toep = jnp.broadcast_to(circ[:, :, None, :], (heads, nro, w, 2 * w)).reshape(heads, nro, 2 * w * w)
    toep = toep[:, :, :w * (2 * w - 1)].reshape(heads, nro, w, 2 * w - 1)[..., :w]
    strips = []
    for qr in range(r):
        lo = NA_ROWS - 1 - r - qr
        strips.append(toep[:, lo:lo + 3 * r].transpose(0, 2, 1, 3).reshape(heads, w, 3 * r * w))
    tile = jnp.concatenate(strips, axis=1)
    return jnp.where(jnp.asarray(_na_valid(rows))[None], tile[:, None], NEG_INF)


def _neighborhood_attention(plain, plain_c, rpb, bsz, seq, q_blk, k_blk, v_blk, heads_per_step):
    rows = seq // GRID_W
    tq = NA_QROWS * GRID_W
    nblk = rows // NA_QROWS
    ctx = plain_c.shape[0] // bsz
    hb = heads_per_step
    hw = hb * HEAD_DIM
    bias = _na_bias_tables(rpb, rows)

    def blk(off, d):
        def imap(b, h, i):
            return (b * nblk + jnp.clip(i + d, 0, nblk - 1), off + h)
        return pl.BlockSpec((tq, hw), imap)

    def bias_map(b, h, i):
        return (h, jnp.where(i == 0, 0, jnp.where(i == nblk - 1, 2, 1)), 0, 0)

    in_specs = [blk(q_blk, 0),
                blk(k_blk, -1), blk(k_blk, 0), blk(k_blk, 1),
                blk(v_blk, -1), blk(v_blk, 0), blk(v_blk, 1),
                pl.BlockSpec((ctx, hw), lambda b, h, i: (b, k_blk + h)),
                pl.BlockSpec((ctx, hw), lambda b, h, i: (b, v_blk + h)),
                pl.BlockSpec((hb, None, tq, 3 * tq), bias_map)]
    return pl.pallas_call(
        _na_body,
        grid=(bsz, C_HEADS // hb, nblk),
        in_specs=in_specs,
        out_specs=pl.BlockSpec((tq, hw), lambda b, h, i: (b * nblk + i, h)),
        out_shape=jax.ShapeDtypeStruct((bsz * seq, C_HEADS * HEAD_DIM), BF16),
        compiler_params=_params(3),
        name="na_attn",
    )(plain, plain, plain, plain, plain, plain, plain, plain_c, plain_c, bias)


def _rope_tables(seq):
    t = jnp.arange(seq, dtype=jnp.int32)
    row = (t // GRID_W).astype(F32)
    col = (t % GRID_W).astype(F32)

    def cs(rot_dim):
        n_freq = rot_dim // 4
        inv = jnp.power(ROPE_BASE, -jnp.arange(n_freq, dtype=F32) / n_freq)
        ang = jnp.concatenate([row[:, None] * inv, col[:, None] * inv], axis=-1)
        return jnp.cos(ang), jnp.sin(ang)

    ca, sa = cs(HEAD_DIM)
    cb, sb = cs(ROPE_DIM)
    zb = jnp.zeros_like(cb)
    return (jnp.concatenate([ca, ca], axis=1), jnp.concatenate([-sa, sa], axis=1),
            jnp.concatenate([cb, zb, cb, zb], axis=1), jnp.concatenate([-sb, zb, sb, zb], axis=1))


def _spread_rope_cols(w):
    h = ROPE_DIM // 2
    z = jnp.zeros(w.shape[:-1] + (LANE // 2 - h,), w.dtype)
    return jnp.concatenate([w[..., :h], z, w[..., h:], z], axis=-1)


def _layer_weights(w_in, w_q_b, w_kv_b):
    kvh = A_KV_HEADS * HEAD_DIM
    ch = C_HEADS * HEAD_DIM
    ah = A_HEADS * HEAD_DIM
    sizes = [kvh, kvh, KV_LORA, ROPE_DIM, ch, ch, ah, Q_LORA, ch, N_BRANCH * D_MODEL]
    offs = np.concatenate([[0], np.cumsum(sizes)])
    a_k, a_v, b_ckv, b_kr, c_k, c_v, a_q, b_qa, c_q, gate = [w_in[:, offs[i]:offs[i + 1]] for i in range(len(sizes))]
    w = {
        "rope": jnp.concatenate([a_k, a_q], axis=1).astype(BF16),
        "plain": jnp.concatenate([a_v, c_k, c_v, c_q], axis=1).astype(BF16),
        "ckv": jnp.concatenate([b_ckv, _spread_rope_cols(b_kr)], axis=1).astype(BF16),
        "bqa": b_qa.astype(BF16),
        "gate": gate.astype(BF16),
        "qb": jnp.concatenate([w_q_b[..., :NOPE_DIM], _spread_rope_cols(w_q_b[..., NOPE_DIM:])], axis=-1)
        .reshape(Q_LORA, B_HEADS * MLA_QK).astype(BF16),
        "uk": w_kv_b[..., :NOPE_DIM].reshape(KV_LORA, B_HEADS * NOPE_DIM).astype(BF16),
        "uv": w_kv_b[..., NOPE_DIM:].reshape(KV_LORA, B_HEADS * V_DIM).astype(BF16),
    }
    return w


def _layer(x2, ctx2, cv, tabs, w_ada, b_ada, g_mix, w_in, sink_a, g_q_a, w_q_b, g_kv_a, w_kv_b, rpb_c, w_br,
           w_out, g_ffn, w_ffn_gate, w_ffn_up, w_ffn_down, last):
    bsz, seq, ctx, d = BATCH, SEQ, CTX_LEN, D_MODEL
    cos_a, sin_a, cos_b, sin_b = tabs
    kvb = A_KV_HEADS * HEAD_DIM // LANE
    chb = C_HEADS * HEAD_DIM // LANE
    na_hb = math.gcd(math.gcd(kvb, chb), 4)

    mod = _adaln(cv, w_ada, b_ada)
    sh_m, sc_m, gt_m, sh_f, sc_f, gt_f = [mod[:bsz, i * d:(i + 1) * d].reshape(bsz, 1, d) for i in range(6)]
    csh_m, csc_m, cgt_m, csh_f, csc_f, cgt_f = [mod[bsz:bsz + 1, i * d:(i + 1) * d].reshape(1, 1, d) for i in range(6)]
    w = _layer_weights(w_in, w_q_b, w_kv_b)
    w_br_b = w_br.astype(BF16)

    hc = _norm_mod(ctx2, g_mix, csh_m, csc_m, ctx)
    ropecols_c = _mm_plain(hc, w["rope"] if not last else w["rope"][:, :kvb * LANE], bm=256, name="mm_plain_c")
    plain_c = _mm_plain(hc, w["plain"] if not last else w["plain"][:, :(kvb + 2 * chb) * LANE], bm=256,
                        name="mm_plain_c")
    ckv_c, kr_c = _mm_ckv(hc, w["ckv"], g_kv_a, None, None, ctx, rope=False, bm=256)

    hx = _norm_mod(x2, g_mix, sh_m, sc_m, seq)
    qk_a = _mm_rope(hx, w["rope"], cos_a, sin_a, seq)
    plain = _mm_plain(hx, w["plain"])
    ckv, kr = _mm_ckv(hx, w["ckv"], g_kv_a, cos_b, sin_b, seq, rope=True)
    bqa = _mm_rms(hx, w["bqa"], g_q_a)
    gates = _mm_sigmoid(hx, w["gate"])
    qb = _mm_qb(bqa, w["qb"], cos_b, sin_b, seq, rope=True)

    lk = ctx + seq
    ckv_all = jnp.concatenate([ckv_c.reshape(bsz, ctx, -1), ckv.reshape(bsz, seq, -1)], axis=1).reshape(bsz * lk, -1)
    kr_all = jnp.concatenate([kr_c.reshape(bsz, ctx, -1), kr.reshape(bsz, seq, -1)], axis=1).reshape(bsz * lk, -1)
    k_exp = _mm_kexp(ckv_all, w["uk"], kr_all).reshape(bsz, lk, -1)
    v_exp = _mm_vexp(ckv_all, w["uv"]).reshape(bsz, lk, -1)

    o_a = _window_attention(sink_a, qk_a, plain, ropecols_c, plain_c, bsz, seq, kvb)
    o_b = _dense_attention(qb.reshape(bsz, seq, -1), k_exp, v_exp, heads=B_HEADS, groups=1, dk=MLA_QK, dv=V_DIM,
                           q_off=0, k_off=0, v_off=0, lk=lk, tq=1024, tk=_mla_chunk(lk), scale=None, ones_col=True,
                           name="mla_attn").reshape(bsz * seq, -1)
    o_c = _neighborhood_attention(plain, plain_c, rpb_c, bsz, seq, q_blk=(kvb + 2 * chb) // na_hb, k_blk=kvb // na_hb,
                                  v_blk=(kvb + chb) // na_hb, heads_per_step=na_hb)

    y = _mm_merge(o_a, o_b, o_c, w_br_b, gates)
    x2 = _mm_resid(y, w_out.astype(BF16), x2, gt_m, seq)
    h2 = _norm_mod(x2, g_ffn, sh_f, sc_f, seq)
    act = _mm_ffn_up(h2, w_ffn_gate.astype(BF16), w_ffn_up.astype(BF16))
    w_down = w_ffn_down.astype(BF16)
    x2 = _mm_resid(act, w_down, x2, gt_f, seq, bm=512, bn=256, name="mm_ffn_down")

    if not last:
        gates_c = _mm_sigmoid(hc, w["gate"], bm=256)
        bqa_c = _mm_rms(hc, w["bqa"], g_q_a, bm=256)
        qb_c = _mm_qb(bqa_c, w["qb"], None, None, ctx, rope=False, bm=256)
        rc3 = ropecols_c.reshape(bsz, ctx, -1)
        pc3 = plain_c.reshape(bsz, ctx, -1)
        oc_a = _dense_attention(rc3, rc3, pc3, heads=A_HEADS, groups=A_HEADS // A_KV_HEADS, dk=HEAD_DIM, dv=HEAD_DIM,
                                q_off=kvb, k_off=0, v_off=0, lk=ctx, tq=ctx, tk=ctx, scale=HEAD_DIM ** -0.5,
                                sink=sink_a, name="ctx_attn_a").reshape(bsz * ctx, -1)
        oc_b = _dense_attention(qb_c.reshape(bsz, ctx, -1), k_exp, v_exp, heads=B_HEADS, groups=1, dk=MLA_QK,
                                dv=V_DIM, q_off=0, k_off=0, v_off=0, lk=ctx, tq=ctx, tk=ctx, scale=None, ones_col=True,
                                name="ctx_attn_b").reshape(bsz * ctx, -1)
        oc_c = _dense_attention(pc3, pc3, pc3, heads=C_HEADS, groups=1, dk=HEAD_DIM, dv=HEAD_DIM,
                                q_off=kvb + 2 * chb, k_off=kvb, v_off=kvb + chb, lk=ctx, tq=ctx, tk=ctx,
                                scale=HEAD_DIM ** -0.5, name="ctx_attn_c").reshape(bsz * ctx, -1)
        yc = _mm_merge(oc_a, oc_b, oc_c, w_br_b, gates_c, bm=256)
        ctx2 = _mm_resid(yc, w_out.astype(BF16), ctx2, cgt_m, ctx, bm=256, name="mm_resid_c")
        hc2 = _norm_mod(ctx2, g_ffn, csh_f, csc_f, ctx)
        act_c = _mm_ffn_up(hc2, w_ffn_gate.astype(BF16), w_ffn_up.astype(BF16), bm=256)
        ctx2 = _mm_resid(act_c, w_down, ctx2, cgt_f, ctx, bm=256, bn=256, name="mm_ffn_down_c")
    return x2, ctx2


def _mla_chunk(lk):
    for parts in range(12, 0, -1):
        if lk % parts == 0 and (lk // parts) % (2 * LANE) == 0:
            return lk // parts
    return lk


def _window_attention(sink, qk_a, plain, ropecols_c, plain_c, bsz, seq, kvb):
    nblk = seq // BLOCK
    groups = A_HEADS // A_KV_HEADS
    gw = groups * HEAD_DIM
    ctx = plain_c.shape[0] // bsz

    def blk(d):
        def imap(b, h, n):
            return (b * nblk + jnp.clip(n + d, 0, nblk - 1), h)
        return pl.BlockSpec((BLOCK, HEAD_DIM), imap)

    cspec = pl.BlockSpec((ctx, HEAD_DIM), lambda b, h, n: (b, h))
    in_specs = [pl.BlockSpec(memory_space=pltpu.SMEM),
                pl.BlockSpec((BLOCK, gw), lambda b, h, n: (b * nblk + n, kvb // groups + h)),
                blk(-1), blk(0), blk(1), blk(-1), blk(0), blk(1), cspec, cspec]
    return pl.pallas_call(
        functools.partial(_win_body, groups=groups),
        grid=(bsz, A_KV_HEADS, nblk),
        in_specs=in_specs,
        out_specs=pl.BlockSpec((BLOCK, gw), lambda b, h, n: (b * nblk + n, h)),
        out_shape=jax.ShapeDtypeStruct((bsz * seq, A_HEADS * HEAD_DIM), BF16),
        compiler_params=_params(3),
        name="win_attn",
    )(sink, qk_a, qk_a, qk_a, qk_a, plain, plain, plain, ropecols_c, plain_c)


def kernel(x, c, ctx, c_ctx, w_ada, b_ada, g_mix, w_in, sink_a, g_q_a, w_q_b, g_kv_a, w_kv_b, rpb_c, w_br, w_out,
           g_ffn, w_ffn_gate, w_ffn_up, w_ffn_down, g_final):
    bsz, seq, d = x.shape
    tabs = _rope_tables(seq)
    pad = (-(bsz + 1)) % 8
    cv = jnp.concatenate([c, c_ctx[None, :], jnp.zeros((pad, d), F32)], axis=0)
    x2 = x.reshape(bsz * seq, d)
    ctx2 = ctx.reshape(bsz * ctx.shape[1], d)
    for l in range(DEPTH):
        x2, ctx2 = _layer(x2, ctx2, cv, tabs, w_ada[l], b_ada[l], g_mix[l], w_in[l], sink_a[l], g_q_a[l], w_q_b[l],
                          g_kv_a[l], w_kv_b[l], rpb_c[l], w_br[l], w_out[l], g_ffn[l], w_ffn_gate[l], w_ffn_up[l],
                          w_ffn_down[l], last=(l == DEPTH - 1))
    zero = jnp.zeros((1, 1, d), F32)
    out = _norm_mod(x2, g_final, zero, zero, seq, out_dtype=F32)
    return out.reshape(bsz, seq, d)
```

```python
import functools
import math

import numpy as np
import jax
import jax.numpy as jnp
from jax import lax
from jax.experimental import pallas as pl
from jax.experimental.pallas import tpu as pltpu

D_MODEL = 4096
BATCH = 2
SEQ = 8192
DEPTH = 2
GRID_W = 64
CTX_LEN = 256
HEAD_DIM = 128
A_HEADS = 16
A_KV_HEADS = 4
WINDOW = 128
BLOCK = 128
B_HEADS = 16
Q_LORA = 1024
KV_LORA = 512
NOPE_DIM = 128
ROPE_DIM = 64
V_DIM = 128
C_HEADS = 16
NA_ROWS = 8
NA_COLS = 16
N_BRANCH = 3
ROPE_BASE = 10000.0
EPS = 1e-6
NEG_INF = -1e30

LANE = 128
MLA_QK = 2 * LANE
LOG2E = math.log2(math.e)
MLA_LOGIT_SCALE = (NOPE_DIM + ROPE_DIM) ** -0.5 * LOG2E
HEAD_LOGIT_SCALE = HEAD_DIM ** -0.5 * LOG2E
NA_QROWS = 4
VMEM_LIMIT = 56 * 1024 * 1024

BF16 = jnp.bfloat16
F32 = jnp.float32


def _params(n_axes):
    return pltpu.CompilerParams(dimension_semantics=("parallel",) * n_axes,
                                vmem_limit_bytes=VMEM_LIMIT)


def _dot(a, b):
    return jnp.dot(a, b, preferred_element_type=F32)


def _dot_nt(a, b):
    return lax.dot_general(a, b, (((1,), (1,)), ((), ())), preferred_element_type=F32)


def _tile(n, pref):
    if n <= pref:
        return n
    t = pref - pref % LANE
    while t >= LANE:
        if n % t == 0:
            return t
        t -= LANE
    return n


def _adaln_body(c_ref, w_ref, b_ref, o_ref):
    cv = c_ref[...]
    o_ref[...] = _dot(cv * jax.nn.sigmoid(cv), w_ref[...]) + b_ref[...]


def _adaln(cv, w_ada, b_ada, layer):
    rows, d = cv.shape
    n = w_ada.shape[2]
    bn = _tile(n, 512)
    return pl.pallas_call(
        _adaln_body,
        grid=(n // bn,),
        in_specs=[pl.BlockSpec((rows, d), lambda j: (0, 0)),
                  pl.BlockSpec((None, d, bn), lambda j: (layer, 0, j)),
                  pl.BlockSpec((None, 1, bn), lambda j: (layer, 0, j))],
        out_specs=pl.BlockSpec((rows, bn), lambda j: (0, j)),
        out_shape=jax.ShapeDtypeStruct((rows, n), F32),
        compiler_params=_params(1),
        name="adaln",
    )(cv, w_ada, b_ada.reshape(b_ada.shape[0], 1, n))


def _norm_mod_body(x_ref, g_ref, sh_ref, sc_ref, o_ref):
    x = x_ref[...]
    y = x * lax.rsqrt(jnp.mean(x * x, axis=-1, keepdims=True) + EPS) * g_ref[...]
    o_ref[...] = (y * (1.0 + sc_ref[...]) + sh_ref[...]).astype(o_ref.dtype)


def _norm_mod(x2, g, shift, scale, rows_per_batch, out_dtype=BF16):
    m, d = x2.shape
    ts = _tile(rows_per_batch, 256)
    tpb = rows_per_batch // ts
    per_batch = shift.shape[0] > 1
    mod_spec = pl.BlockSpec((None, 1, d), (lambda i: (i // tpb, 0, 0)) if per_batch else (lambda i: (0, 0, 0)))
    return pl.pallas_call(
        _norm_mod_body,
        grid=(m // ts,),
        in_specs=[pl.BlockSpec((ts, d), lambda i: (i, 0)),
                  pl.BlockSpec((1, d), lambda i: (0, 0)),
                  mod_spec, mod_spec],
        out_specs=pl.BlockSpec((ts, d), lambda i: (i, 0)),
        out_shape=jax.ShapeDtypeStruct((m, d), out_dtype),
        compiler_params=_params(1),
        name="norm_mod",
    )(x2, g.reshape(1, d), shift, scale)


def _a_spec(bm, k):
    return pl.BlockSpec((bm, k), lambda i, j: (i, 0))


class _LayerWeight:
    def __init__(self, arr, layer, ncols=None):
        self.arr, self.layer = arr, layer
        self.shape = (arr.shape[1], arr.shape[2] if ncols is None else ncols)

    def cols(self, ncols):
        return _LayerWeight(self.arr, self.layer, ncols)


def _w_spec(w, bn):
    return pl.BlockSpec((None, w.shape[0], bn), lambda i, j: (w.layer, 0, j))


def _mn_spec(bm, bn):
    return pl.BlockSpec((bm, bn), lambda i, j: (i, j))


def _pos_spec(bm, width, tiles_per_batch):
    return pl.BlockSpec((bm, width), lambda i, j: (i % tiles_per_batch, 0))


def _mm_call(body, grid, in_specs, out_specs, out_shape, args, name):
    args = [a.arr if isinstance(a, _LayerWeight) else a for a in args]
    return pl.pallas_call(body, grid=grid, in_specs=in_specs, out_specs=out_specs, out_shape=out_shape,
                          compiler_params=_params(2), name=name)(*args)


def _plain_body(a_ref, w_ref, o_ref):
    o_ref[...] = _dot(a_ref[...], w_ref[...]).astype(o_ref.dtype)


def _scaled_body(a_ref, w_ref, cs_ref, o_ref):
    o_ref[...] = (_dot(a_ref[...], w_ref[...]) * cs_ref[...]).astype(o_ref.dtype)


def _n_spec(bn):
    return pl.BlockSpec((1, bn), lambda i, j: (0, j))


def _mm_plain(a, w, colscale=None, bm=1024, bn=512, name="mm_plain"):
    m, k = a.shape
    n = w.shape[1]
    bm, bn = _tile(m, bm), _tile(n, bn)
    if colscale is None:
        return _mm_call(_plain_body, (m // bm, n // bn), [_a_spec(bm, k), _w_spec(w, bn)], _mn_spec(bm, bn),
                        jax.ShapeDtypeStruct((m, n), BF16), (a, w), name)
    return _mm_call(_scaled_body, (m // bm, n // bn), [_a_spec(bm, k), _w_spec(w, bn), _n_spec(bn)],
                    _mn_spec(bm, bn), jax.ShapeDtypeStruct((m, n), BF16), (a, w, colscale), name)


def _rope_chunks(acc, cos, sin):
    outs = []
    for c in range(acc.shape[1] // LANE):
        t = acc[:, c * LANE:(c + 1) * LANE]
        outs.append(t * cos + pltpu.roll(t, LANE // 2, 1) * sin)
    return outs[0] if len(outs) == 1 else jnp.concatenate(outs, axis=1)


def _rope_body(a_ref, w_ref, cs_ref, cos_ref, sin_ref, o_ref):
    acc = _dot(a_ref[...], w_ref[...]) * cs_ref[...]
    o_ref[...] = _rope_chunks(acc, cos_ref[...], sin_ref[...]).astype(o_ref.dtype)


def _mm_rope(a, w, colscale, cos2, sin2, rows_per_batch, bm=1024, bn=512):
    m, k = a.shape
    n = w.shape[1]
    bm, bn = _tile(rows_per_batch, bm), _tile(n, bn)
    tpb = rows_per_batch // bm
    return _mm_call(_rope_body, (m // bm, n // bn),
                    [_a_spec(bm, k), _w_spec(w, bn), _n_spec(bn), _pos_spec(bm, LANE, tpb), _pos_spec(bm, LANE, tpb)],
                    _mn_spec(bm, bn), jax.ShapeDtypeStruct((m, n), BF16), (a, w, colscale, cos2, sin2), "mm_rope")


def _ckv_body(a_ref, w_ref, g_ref, *rest, rope):
    if rope:
        cos_ref, sin_ref, ckv_ref, kr_ref = rest
    else:
        ckv_ref, kr_ref = rest
    acc = _dot(a_ref[...], w_ref[...])
    nl = ckv_ref.shape[1]
    t = acc[:, :nl]
    ckv_ref[...] = (t * lax.rsqrt(jnp.mean(t * t, axis=-1, keepdims=True) + EPS) * g_ref[...]).astype(ckv_ref.dtype)
    r = acc[:, nl:]
    if rope:
        r = _rope_chunks(r, cos_ref[...], sin_ref[...])
    kr_ref[...] = r.astype(kr_ref.dtype)


def _mm_ckv(a, w, g, cos2, sin2, rows_per_batch, rope, bm=1024):
    m, k = a.shape
    n = w.shape[1]
    nl = n - LANE
    bm = _tile(rows_per_batch, bm)
    tpb = rows_per_batch // bm
    in_specs = [_a_spec(bm, k), _w_spec(w, n), pl.BlockSpec((1, nl), lambda i, j: (0, 0))]
    args = [a, w, g.reshape(1, nl)]
    if rope:
        in_specs += [_pos_spec(bm, LANE, tpb), _pos_spec(bm, LANE, tpb)]
        args += [cos2, sin2]
    return _mm_call(functools.partial(_ckv_body, rope=rope), (m // bm, 1), in_specs,
                    [pl.BlockSpec((bm, nl), lambda i, j: (i, 0)), pl.BlockSpec((bm, LANE), lambda i, j: (i, 0))],
                    [jax.ShapeDtypeStruct((m, nl), BF16), jax.ShapeDtypeStruct((m, LANE), BF16)], args, "mm_ckv")


def _rms_body(a_ref, w_ref, g_ref, o_ref):
    t = _dot(a_ref[...], w_ref[...])
    o_ref[...] = (t * lax.rsqrt(jnp.mean(t * t, axis=-1, keepdims=True) + EPS) * g_ref[...]).astype(o_ref.dtype)


def _mm_rms(a, w, g, bm=1024):
    m, k = a.shape
    n = w.shape[1]
    bm = _tile(m, bm)
    return _mm_call(_rms_body, (m // bm, 1), [_a_spec(bm, k), _w_spec(w, n), pl.BlockSpec((1, n), lambda i, j: (0, 0))],
                    _mn_spec(bm, n), jax.ShapeDtypeStruct((m, n), BF16), (a, w, g.reshape(1, n)), "mm_rms")


def _sigmoid_body(a_ref, w_ref, o_ref):
    o_ref[...] = jax.nn.sigmoid(_dot(a_ref[...], w_ref[...]))


def _mm_sigmoid(a, w, bm=1024, bn=1024):
    m, k = a.shape
    n = w.shape[1]
    bm, bn = _tile(m, bm), _tile(n, bn)
    return _mm_call(_sigmoid_body, (m // bm, n // bn), [_a_spec(bm, k), _w_spec(w, bn)], _mn_spec(bm, bn),
                    jax.ShapeDtypeStruct((m, n), F32), (a, w), "mm_gates")


def _qb_body(a_ref, w_ref, *rest, rope):
    if rope:
        cos_ref, sin_ref, o_ref = rest
    else:
        (o_ref,) = rest
    acc = _dot(a_ref[...], w_ref[...]) * MLA_LOGIT_SCALE
    if rope:
        parts = []
        for h in range(acc.shape[1] // MLA_QK):
            parts.append(acc[:, h * MLA_QK:h * MLA_QK + NOPE_DIM])
            parts.append(_rope_chunks(acc[:, h * MLA_QK + NOPE_DIM:(h + 1) * MLA_QK], cos_ref[...], sin_ref[...]))
        acc = jnp.concatenate(parts, axis=1)
    o_ref[...] = acc.astype(o_ref.dtype)


def _mm_qb(a, w, cos2, sin2, rows_per_batch, rope, bm=1024):
    m, k = a.shape
    n = w.shape[1]
    bm = _tile(rows_per_batch, bm)
    bn = 2 * MLA_QK if n % (2 * MLA_QK) == 0 else MLA_QK
    tpb = rows_per_batch // bm
    in_specs = [_a_spec(bm, k), _w_spec(w, bn)]
    args = [a, w]
    if rope:
        in_specs += [_pos_spec(bm, LANE, tpb), _pos_spec(bm, LANE, tpb)]
        args += [cos2, sin2]
    return _mm_call(functools.partial(_qb_body, rope=rope), (m // bm, n // bn), in_specs, _mn_spec(bm, bn),
                    jax.ShapeDtypeStruct((m, n), BF16), args, "mm_qb")


def _kexp_body(a_ref, w_ref, kr_ref, o_ref):
    acc = _dot(a_ref[...], w_ref[...]).astype(o_ref.dtype)
    kr = kr_ref[...]
    parts = []
    for h in range(acc.shape[1] // NOPE_DIM):
        parts += [acc[:, h * NOPE_DIM:(h + 1) * NOPE_DIM], kr]
    o_ref[...] = jnp.concatenate(parts, axis=1)


def _mm_kexp(ckv, w_uk, kr, bm=512, heads_per_tile=4):
    m, k = ckv.shape
    n = w_uk.shape[1]
    heads = n // NOPE_DIM
    bm = _tile(m, bm)
    hpt = heads_per_tile if heads % heads_per_tile == 0 else 1
    return _mm_call(_kexp_body, (m // bm, heads // hpt),
                    [_a_spec(bm, k), _w_spec(w_uk, hpt * NOPE_DIM), pl.BlockSpec((bm, LANE), lambda i, j: (i, 0))],
                    _mn_spec(bm, hpt * MLA_QK), jax.ShapeDtypeStruct((m, heads * MLA_QK), BF16),
                    (ckv, w_uk, kr), "mm_kexp")


def _vexp_body(a_ref, w_ref, o_ref):
    acc = _dot(a_ref[...], w_ref[...]).astype(o_ref.dtype)
    lane = lax.broadcasted_iota(jnp.int32, (acc.shape[0], V_DIM), 1)
    ones = jnp.where(lane == 0, 1.0, 0.0).astype(o_ref.dtype)
    parts = []
    for h in range(acc.shape[1] // V_DIM):
        parts += [acc[:, h * V_DIM:(h + 1) * V_DIM], ones]
    o_ref[...] = jnp.concatenate(parts, axis=1)


def _mm_vexp(ckv, w_uv, bm=512, heads_per_tile=4):
    m, k = ckv.shape
    heads = w_uv.shape[1] // V_DIM
    bm = _tile(m, bm)
    hpt = heads_per_tile if heads % heads_per_tile == 0 else 1
    return _mm_call(_vexp_body, (m // bm, heads // hpt), [_a_spec(bm, k), _w_spec(w_uv, hpt * V_DIM)],
                    _mn_spec(bm, 2 * hpt * V_DIM), jax.ShapeDtypeStruct((m, 2 * heads * V_DIM), BF16),
                    (ckv, w_uv), "mm_vexp")


def _merge_body(oa_ref, ob_ref, oc_ref, wa_ref, wb_ref, wc_ref, ga_ref, gb_ref, gc_ref, o_ref):
    y = ga_ref[...] * _dot(oa_ref[...], wa_ref[...])
    y = y + gb_ref[...] * _dot(ob_ref[...], wb_ref[...])
    y = y + gc_ref[...] * _dot(oc_ref[...], wc_ref[...])
    o_ref[...] = y.astype(o_ref.dtype)


def _mm_merge(o_a, o_b, o_c, w_br, layer, gates, bm=1024, bn=256):
    m, k = o_a.shape
    n = w_br.shape[3]
    bm, bn = _tile(m, bm), _tile(n, bn)
    nb = n // bn
    wspec = [pl.BlockSpec((None, None, k, bn), (lambda i, j, r=r: (layer, r, 0, j))) for r in range(N_BRANCH)]
    gspec = [pl.BlockSpec((bm, bn), (lambda i, j, r=r: (i, r * nb + j))) for r in range(N_BRANCH)]
    return _mm_call(_merge_body, (m // bm, nb), [_a_spec(bm, k)] * 3 + wspec + gspec, _mn_spec(bm, bn),
                    jax.ShapeDtypeStruct((m, n), BF16), (o_a, o_b, o_c, w_br, w_br, w_br, gates, gates, gates),
                    "mm_merge")


def _resid_body(a_ref, w_ref, x_ref, gt_ref, o_ref):
    o_ref[...] = x_ref[...] + gt_ref[...] * _dot(a_ref[...], w_ref[...])


def _mm_resid(a, w, x2, gate, rows_per_batch, bm=1024, bn=512, name="mm_resid"):
    m, k = a.shape
    n = w.shape[1]
    bm, bn = _tile(rows_per_batch, bm), _tile(n, bn)
    tpb = rows_per_batch // bm
    per_batch = gate.shape[0] > 1
    gspec = pl.BlockSpec((None, 1, bn), (lambda i, j: (i // tpb, 0, j)) if per_batch else (lambda i, j: (0, 0, j)))
    return _mm_call(_resid_body, (m // bm, n // bn), [_a_spec(bm, k), _w_spec(w, bn), _mn_spec(bm, bn), gspec],
                    _mn_spec(bm, bn), jax.ShapeDtypeStruct((m, n), F32), (a, w, x2, gate), name)


def _ffn_up_body(a_ref, wg_ref, wu_ref, o_ref):
    a = a_ref[...]
    g = _dot(a, wg_ref[...])
    o_ref[...] = (g * jax.nn.sigmoid(g) * _dot(a, wu_ref[...])).astype(o_ref.dtype)


def _mm_ffn_up(a, wg, wu, bm=2048, bn=256):
    m, k = a.shape
    n = wg.shape[1]
    bm, bn = _tile(m, bm), _tile(n, bn)
    return _mm_call(_ffn_up_body, (m // bm, n // bn), [_a_spec(bm, k), _w_spec(wg, bn), _w_spec(wu, bn)],
                    _mn_spec(bm, bn), jax.ShapeDtypeStruct((m, n), BF16), (a, wg, wu), "mm_ffn_up")


def _with_ones_col(v):
    lane = lax.broadcasted_iota(jnp.int32, v.shape, 1)
    return jnp.concatenate([v, jnp.where(lane == 0, 1.0, 0.0).astype(v.dtype)], axis=1)


def _win_body(sink_ref, q_ref, kp_ref, kc_ref, kn_ref, vp_ref, vc_ref, vn_ref, kx_ref, vx_ref, mask_ref, o_ref, *,
              groups):
    kvh = pl.program_id(1)
    k_all = jnp.concatenate([kp_ref[...], kc_ref[...], kn_ref[...], kx_ref[...]], axis=0)
    v_all = _with_ones_col(jnp.concatenate([vp_ref[...], vc_ref[...], vn_ref[...], vx_ref[...]], axis=0))
    mask = mask_ref[...]
    for g in range(groups):
        q = q_ref[:, g * HEAD_DIM:(g + 1) * HEAD_DIM]
        s = _dot_nt(q, k_all) + mask
        sink = sink_ref[kvh * groups + g] * LOG2E
        m = jnp.maximum(jnp.max(s, axis=1, keepdims=True), sink)
        acc = _dot(jnp.exp2(s - m).astype(BF16), v_all)
        l = acc[:, HEAD_DIM:HEAD_DIM + 1] + jnp.exp2(sink - m)
        o_ref[:, g * HEAD_DIM:(g + 1) * HEAD_DIM] = (acc[:, :HEAD_DIM] / l).astype(o_ref.dtype)


def _win_masks(ctx):
    qi = np.arange(BLOCK)[:, None]
    kj = np.arange(3 * BLOCK)[None, :]
    band = np.abs(kj - qi - BLOCK) <= WINDOW
    out = []
    for lo, hi in ((BLOCK, 3 * BLOCK), (0, 3 * BLOCK), (0, 2 * BLOCK)):
        ok = band & (kj >= lo) & (kj < hi)
        out.append(np.concatenate([np.where(ok, 0.0, NEG_INF), np.zeros((BLOCK, ctx))], axis=1))
    return np.stack(out).astype(np.float32)


def _dense_body(*refs, tk, scale, has_sink, ones_col):
    if has_sink:
        sink_ref, q_ref, k_ref, v_ref, o_ref = refs
    else:
        q_ref, k_ref, v_ref, o_ref = refs
    q = q_ref[...]
    lk = k_ref.shape[0]
    dv = o_ref.shape[1]
    exp = jnp.exp2 if scale is None else jnp.exp
    m = l = acc = None
    for c in range(lk // tk):
        s = _dot_nt(q, k_ref[c * tk:(c + 1) * tk, :])
        if scale is not None:
            s = s * scale
        mc = jnp.max(s, axis=1, keepdims=True)
        if c == 0:
            m_new = mc
            if has_sink:
                sink = sink_ref[pl.program_id(1)]
                m_new = jnp.maximum(m_new, sink)
        else:
            m_new = jnp.maximum(m, mc)
        p = exp(s - m_new)
        pv = _dot(p.astype(BF16), v_ref[c * tk:(c + 1) * tk, :])
        if not ones_col:
            ps = jnp.sum(p, axis=1, keepdims=True)
            if c == 0:
                l = ps + exp(sink - m_new) if has_sink else ps
            else:
                l = exp(m - m_new) * l + ps
        acc = pv if c == 0 else exp(m - m_new) * acc + pv
        m = m_new
    if ones_col:
        l = acc[:, dv:dv + 1]
        acc = acc[:, :dv]
    o_ref[...] = (acc / l).astype(o_ref.dtype)


def _dense_attention(q3, k3, v3, *, heads, groups, dk, dv, q_off, k_off, v_off, lk, tq, tk, scale, sink=None,
                     ones_col=False, name="dense_attn"):
    assert not (ones_col and sink is not None)
    bsz, lq, _ = q3.shape
    tq = _tile(lq, tq)
    dvw = 2 * dv if ones_col else dv
    in_specs = [pl.BlockSpec((None, tq, dk), lambda b, h, i: (b, i, q_off + h)),
                pl.BlockSpec((None, lk, dk), lambda b, h, i: (b, 0, k_off + h // groups)),
                pl.BlockSpec((None, lk, dvw), lambda b, h, i: (b, 0, v_off + h // groups))]
    args = [q3, k3, v3]
    if sink is not None:
        in_specs = [pl.BlockSpec(memory_space=pltpu.SMEM)] + in_specs
        args = [sink] + args
    return pl.pallas_call(
        functools.partial(_dense_body, tk=tk, scale=scale, has_sink=sink is not None, ones_col=ones_col),
        grid=(bsz, heads, lq // tq),
        in_specs=in_specs,
        out_specs=pl.BlockSpec((None, tq, dv), lambda b, h, i: (b, i, h)),
        out_shape=jax.ShapeDtypeStruct((bsz, lq, heads * dv), BF16),
        compiler_params=_params(3),
        name=name,
    )(*args)


def _na_body(q_ref, kp_ref, kc_ref, kn_ref, vp_ref, vc_ref, vn_ref, kx_ref, vx_ref, bias_ref, o_ref):
    for h in range(bias_ref.shape[0]):
        cols = slice(h * HEAD_DIM, (h + 1) * HEAD_DIM)
        k_all = jnp.concatenate([kp_ref[:, cols], kc_ref[:, cols], kn_ref[:, cols], kx_ref[:, cols]], axis=0)
        v_all = _with_ones_col(
            jnp.concatenate([vp_ref[:, cols], vc_ref[:, cols], vn_ref[:, cols], vx_ref[:, cols]], axis=0))
        s = _dot_nt(q_ref[:, cols], k_all) + bias_ref[h]
        m = jnp.max(s, axis=1, keepdims=True)
        acc = _dot(jnp.exp2(s - m).astype(BF16), v_all)
        o_ref[:, cols] = (acc[:, :HEAD_DIM] / acc[:, HEAD_DIM:HEAD_DIM + 1]).astype(o_ref.dtype)


def _na_valid(rows):
    r = NA_QROWS
    nblk = rows // r
    assert rows % r == 0 and nblk >= 3 and rows >= NA_ROWS and NA_ROWS >= 2 * r
    kwin = min(NA_ROWS, rows)
    valid = []
    for i in (0, 1, nblk - 1):
        qrow = (r * i + np.arange(r))[:, None, None, None]
        qcol = np.arange(GRID_W)[None, :, None, None]
        krow = (r * (i - 1) + np.arange(3 * r))[None, None, :, None]
        kcol = np.arange(GRID_W)[None, None, None, :]
        r0 = np.clip(qrow - kwin // 2, 0, rows - kwin)
        c0 = np.clip(qcol - NA_COLS // 2, 0, GRID_W - NA_COLS)
        ok = (krow >= 0) & (krow < rows) & (krow >= r0) & (krow < r0 + kwin) & (kcol >= c0) & (kcol < c0 + NA_COLS)
        valid.append(ok.reshape(r * GRID_W, 3 * r * GRID_W))
    return np.stack(valid)


def _na_bias_tables(rpb, rows, ctx):
    r, w = NA_QROWS, GRID_W
    heads, nro, nco = rpb.shape
    circ = jnp.concatenate([rpb[..., NA_COLS - 1:], jnp.zeros((heads, nro, 2 * w - nco), rpb.dtype),
                            rpb[..., :NA_COLS - 1]], axis=-1)
    toep = jnp.broadcast_to(circ[:, :, None, :], (heads, nro, w, 2 * w)).reshape(heads, nro, 2 * w * w)
    toep = toep[:, :, :w * (2 * w - 1)].reshape(heads, nro, w, 2 * w - 1)[..., :w]
    strips = []
    for qr in range(r):
        lo = NA_ROWS - 1 - r - qr
        strips.append(toep[:, lo:lo + 3 * r].transpose(0, 2, 1, 3).reshape(heads, w, 3 * r * w))
    tile = jnp.concatenate(strips, axis=1) * LOG2E
    local = jnp.where(jnp.asarray(_na_valid(rows))[None], tile[:, None], NEG_INF)
    return jnp.concatenate([local, jnp.zeros(local.shape[:3] + (ctx,), local.dtype)], axis=-1)


def _neighborhood_attention(plain, plain_c, rpb, bsz, seq, q_blk, k_blk, v_blk, heads_per_step):
    rows = seq // GRID_W
    tq = NA_QROWS * GRID_W
    nblk = rows // NA_QROWS
    ctx = plain_c.shape[0] // bsz
    hb = heads_per_step
    hw = hb * HEAD_DIM
    bias = _na_bias_tables(rpb, rows, ctx)

    def blk(off, d):
        def imap(b, h, i):
            return (b * nblk + jnp.clip(i + d, 0, nblk - 1), off + h)
        return pl.BlockSpec((tq, hw), imap)

    def bias_map(b, h, i):
        return (h, jnp.where(i == 0, 0, jnp.where(i == nblk - 1, 2, 1)), 0, 0)

    in_specs = [blk(q_blk, 0),
                blk(k_blk, -1), blk(k_blk, 0), blk(k_blk, 1),
                blk(v_blk, -1), blk(v_blk, 0), blk(v_blk, 1),
                pl.BlockSpec((ctx, hw), lambda b, h, i: (b, k_blk + h)),
                pl.BlockSpec((ctx, hw), lambda b, h, i: (b, v_blk + h)),
                pl.BlockSpec((hb, None, tq, 3 * tq + ctx), bias_map)]
    return pl.pallas_call(
        _na_body,
        grid=(bsz, C_HEADS // hb, nblk),
        in_specs=in_specs,
        out_specs=pl.BlockSpec((tq, hw), lambda b, h, i: (b * nblk + i, h)),
        out_shape=jax.ShapeDtypeStruct((bsz * seq, C_HEADS * HEAD_DIM), BF16),
        compiler_params=_params(3),
        name="na_attn",
    )(plain, plain, plain, plain, plain, plain, plain, plain_c, plain_c, bias)


def _rope_tables(seq):
    t = jnp.arange(seq, dtype=jnp.int32)
    row = (t // GRID_W).astype(F32)
    col = (t % GRID_W).astype(F32)

    def cs(rot_dim):
        n_freq = rot_dim // 4
        inv = jnp.power(ROPE_BASE, -jnp.arange(n_freq, dtype=F32) / n_freq)
        ang = jnp.concatenate([row[:, None] * inv, col[:, None] * inv], axis=-1)
        return jnp.cos(ang), jnp.sin(ang)

    ca, sa = cs(HEAD_DIM)
    cb, sb = cs(ROPE_DIM)
    zb = jnp.zeros_like(cb)
    return (jnp.concatenate([ca, ca], axis=1), jnp.concatenate([-sa, sa], axis=1),
            jnp.concatenate([cb, zb, cb, zb], axis=1), jnp.concatenate([-sb, zb, sb, zb], axis=1))


def _spread_rope_cols(w):
    h = ROPE_DIM // 2
    z = jnp.zeros(w.shape[:-1] + (LANE // 2 - h,), w.dtype)
    return jnp.concatenate([w[..., :h], z, w[..., h:], z], axis=-1)


def _stacked_weights(w_in, w_q_b, w_kv_b, w_br, w_out, w_ffn_gate, w_ffn_up, w_ffn_down):
    nl = w_in.shape[0]
    kvh = A_KV_HEADS * HEAD_DIM
    ch = C_HEADS * HEAD_DIM
    ah = A_HEADS * HEAD_DIM
    sizes = [kvh, kvh, KV_LORA, ROPE_DIM, ch, ch, ah, Q_LORA, ch, N_BRANCH * D_MODEL]
    offs = np.concatenate([[0], np.cumsum(sizes)])
    a_k, a_v, b_ckv, b_kr, c_k, c_v, a_q, b_qa, c_q, gate = [w_in[..., offs[i]:offs[i + 1]]
                                                             for i in range(len(sizes))]
    return {
        "rope": jnp.concatenate([a_k, a_q], axis=-1).astype(BF16),
        "plain": jnp.concatenate([a_v, c_k, c_v, c_q], axis=-1).astype(BF16),
        "ckv": jnp.concatenate([b_ckv, _spread_rope_cols(b_kr)], axis=-1).astype(BF16),
        "bqa": b_qa.astype(BF16),
        "gate": gate.astype(BF16),
        "qb": jnp.concatenate([w_q_b[..., :NOPE_DIM], _spread_rope_cols(w_q_b[..., NOPE_DIM:])], axis=-1)
        .reshape(nl, Q_LORA, B_HEADS * MLA_QK).astype(BF16),
        "uk": w_kv_b[..., :NOPE_DIM].reshape(nl, KV_LORA, B_HEADS * NOPE_DIM).astype(BF16),
        "uv": w_kv_b[..., NOPE_DIM:].reshape(nl, KV_LORA, B_HEADS * V_DIM).astype(BF16),
        "br": w_br.astype(BF16),
        "out": w_out.astype(BF16),
        "ffn_gate": w_ffn_gate.astype(BF16),
        "ffn_up": w_ffn_up.astype(BF16),
        "ffn_down": w_ffn_down.astype(BF16),
    }


def _layer(x2, ctx2, cv, tabs, layer, sw, w_ada, b_ada, g_mix, sink_a, g_q_a, g_kv_a, rpb_c, g_ffn, last):
    bsz, seq, ctx, d = BATCH, SEQ, CTX_LEN, D_MODEL
    cos_a, sin_a, cos_b, sin_b = tabs
    kvb = A_KV_HEADS * HEAD_DIM // LANE
    chb = C_HEADS * HEAD_DIM // LANE
    na_hb = math.gcd(math.gcd(kvb, chb), 4)

    mod = _adaln(cv, w_ada, b_ada, layer)
    sh_m, sc_m, gt_m, sh_f, sc_f, gt_f = [mod[:bsz, i * d:(i + 1) * d].reshape(bsz, 1, d) for i in range(6)]
    csh_m, csc_m, cgt_m, csh_f, csc_f, cgt_f = [mod[bsz:bsz + 1, i * d:(i + 1) * d].reshape(1, 1, d) for i in range(6)]
    w = {name: _LayerWeight(arr, layer) for name, arr in sw.items() if name != "br"}

    hc = _norm_mod(ctx2, g_mix, csh_m, csc_m, ctx)
    ropecols_c = _mm_plain(hc, w["rope"] if not last else w["rope"].cols(kvb * LANE), bm=256, name="mm_plain_c")
    plain_c = _mm_plain(hc, w["plain"] if not last else w["plain"].cols((kvb + 2 * chb) * LANE), bm=256,
                        name="mm_plain_c")
    ckv_c, kr_c = _mm_ckv(hc, w["ckv"], g_kv_a, None, None, ctx, rope=False, bm=256)

    hx = _norm_mod(x2, g_mix, sh_m, sc_m, seq)
    qscale = jnp.full((1, A_HEADS * HEAD_DIM), HEAD_LOGIT_SCALE, F32)
    cs_rope = jnp.concatenate([jnp.ones((1, kvb * LANE), F32), qscale], axis=1)
    cs_plain = jnp.concatenate([jnp.ones((1, (kvb + 2 * chb) * LANE), F32),
                                jnp.full((1, chb * LANE), HEAD_LOGIT_SCALE, F32)], axis=1)
    qk_a = _mm_rope(hx, w["rope"], cs_rope, cos_a, sin_a, seq)
    plain = _mm_plain(hx, w["plain"], cs_plain)
    ckv, kr = _mm_ckv(hx, w["ckv"], g_kv_a, cos_b, sin_b, seq, rope=True)
    bqa = _mm_rms(hx, w["bqa"], g_q_a)
    gates = _mm_sigmoid(hx, w["gate"])
    qb = _mm_qb(bqa, w["qb"], cos_b, sin_b, seq, rope=True)

    lk = ctx + seq
    ckv_all = jnp.concatenate([ckv_c.reshape(bsz, ctx, -1), ckv.reshape(bsz, seq, -1)], axis=1).reshape(bsz * lk, -1)
    kr_all = jnp.concatenate([kr_c.reshape(bsz, ctx, -1), kr.reshape(bsz, seq, -1)], axis=1).reshape(bsz * lk, -1)
    k_exp = _mm_kexp(ckv_all, w["uk"], kr_all).reshape(bsz, lk, -1)
    v_exp = _mm_vexp(ckv_all, w["uv"]).reshape(bsz, lk, -1)

    o_a = _window_attention(sink_a, qk_a, plain, ropecols_c, plain_c, bsz, seq, kvb)
    o_b = _dense_attention(qb.reshape(bsz, seq, -1), k_exp, v_exp, heads=B_HEADS, groups=1, dk=MLA_QK, dv=V_DIM,
                           q_off=0, k_off=0, v_off=0, lk=lk, tq=1024, tk=_mla_chunk(lk), scale=None, ones_col=True,
                           name="mla_attn").reshape(bsz * seq, -1)
    o_c = _neighborhood_attention(plain, plain_c, rpb_c, bsz, seq, q_blk=(kvb + 2 * chb) // na_hb, k_blk=kvb // na_hb,
                                  v_blk=(kvb + chb) // na_hb, heads_per_step=na_hb)

    y = _mm_merge(o_a, o_b, o_c, sw["br"], layer, gates)
    x2 = _mm_resid(y, w["out"], x2, gt_m, seq)
    h2 = _norm_mod(x2, g_ffn, sh_f, sc_f, seq)
    act = _mm_ffn_up(h2, w["ffn_gate"], w["ffn_up"])
    x2 = _mm_resid(act, w["ffn_down"], x2, gt_f, seq, bm=512, bn=256, name="mm_ffn_down")

    if not last:
        gates_c = _mm_sigmoid(hc, w["gate"], bm=256)
        bqa_c = _mm_rms(hc, w["bqa"], g_q_a, bm=256)
        qb_c = _mm_qb(bqa_c, w["qb"], None, None, ctx, rope=False, bm=256)
        rc3 = ropecols_c.reshape(bsz, ctx, -1)
        pc3 = plain_c.reshape(bsz, ctx, -1)
        oc_a = _dense_attention(rc3, rc3, pc3, heads=A_HEADS, groups=A_HEADS // A_KV_HEADS, dk=HEAD_DIM, dv=HEAD_DIM,
                                q_off=kvb, k_off=0, v_off=0, lk=ctx, tq=ctx, tk=ctx, scale=HEAD_DIM ** -0.5,
                                sink=sink_a, name="ctx_attn_a").reshape(bsz * ctx, -1)
        oc_b = _dense_attention(qb_c.reshape(bsz, ctx, -1), k_exp, v_exp, heads=B_HEADS, groups=1, dk=MLA_QK,
                                dv=V_DIM, q_off=0, k_off=0, v_off=0, lk=ctx, tq=ctx, tk=ctx, scale=None, ones_col=True,
                                name="ctx_attn_b").reshape(bsz * ctx, -1)
        oc_c = _dense_attention(pc3, pc3, pc3, heads=C_HEADS, groups=1, dk=HEAD_DIM, dv=HEAD_DIM,
                                q_off=kvb + 2 * chb, k_off=kvb, v_off=kvb + chb, lk=ctx, tq=ctx, tk=ctx,
                                scale=HEAD_DIM ** -0.5, name="ctx_attn_c").reshape(bsz * ctx, -1)
        yc = _mm_merge(oc_a, oc_b, oc_c, sw["br"], layer, gates_c, bm=256, bn=512)
        ctx2 = _mm_resid(yc, w["out"], ctx2, cgt_m, ctx, bm=256, name="mm_resid_c")
        hc2 = _norm_mod(ctx2, g_ffn, csh_f, csc_f, ctx)
        act_c = _mm_ffn_up(hc2, w["ffn_gate"], w["ffn_up"], bm=256)
        ctx2 = _mm_resid(act_c, w["ffn_down"], ctx2, cgt_f, ctx, bm=256, bn=256, name="mm_ffn_down_c")
    return x2, ctx2


def _mla_chunk(lk):
    for parts in range(40, 0, -1):
        if lk % parts == 0 and (lk // parts) % (2 * LANE) == 0:
            return lk // parts
    return lk


def _window_attention(sink, qk_a, plain, ropecols_c, plain_c, bsz, seq, kvb):
    nblk = seq // BLOCK
    groups = A_HEADS // A_KV_HEADS
    gw = groups * HEAD_DIM
    ctx = plain_c.shape[0] // bsz

    def blk(d):
        def imap(b, h, n):
            return (b * nblk + jnp.clip(n + d, 0, nblk - 1), h)
        return pl.BlockSpec((BLOCK, HEAD_DIM), imap)

    cspec = pl.BlockSpec((ctx, HEAD_DIM), lambda b, h, n: (b, h))
    mspec = pl.BlockSpec((None, BLOCK, 3 * BLOCK + ctx),
                         lambda b, h, n: (jnp.where(n == 0, 0, jnp.where(n == nblk - 1, 2, 1)), 0, 0))
    in_specs = [pl.BlockSpec(memory_space=pltpu.SMEM),
                pl.BlockSpec((BLOCK, gw), lambda b, h, n: (b * nblk + n, kvb // groups + h)),
                blk(-1), blk(0), blk(1), blk(-1), blk(0), blk(1), cspec, cspec, mspec]
    return pl.pallas_call(
        functools.partial(_win_body, groups=groups),
        grid=(bsz, A_KV_HEADS, nblk),
        in_specs=in_specs,
        out_specs=pl.BlockSpec((BLOCK, gw), lambda b, h, n: (b * nblk + n, h)),
        out_shape=jax.ShapeDtypeStruct((bsz * seq, A_HEADS * HEAD_DIM), BF16),
        compiler_params=_params(3),
        name="win_attn",
    )(sink, qk_a, qk_a, qk_a, qk_a, plain, plain, plain, ropecols_c, plain_c, jnp.asarray(_win_masks(ctx)))


def kernel(x, c, ctx, c_ctx, w_ada, b_ada, g_mix, w_in, sink_a, g_q_a, w_q_b, g_kv_a, w_kv_b, rpb_c, w_br, w_out,
           g_ffn, w_ffn_gate, w_ffn_up, w_ffn_down, g_final):
    bsz, seq, d = x.shape
    tabs = _rope_tables(seq)
    pad = (-(bsz + 1)) % 8
    cv = jnp.concatenate([c, c_ctx[None, :], jnp.zeros((pad, d), F32)], axis=0)
    x2 = x.reshape(bsz * seq, d)
    ctx2 = ctx.reshape(bsz * ctx.shape[1], d)
    sw = _stacked_weights(w_in, w_q_b, w_kv_b, w_br, w_out, w_ffn_gate, w_ffn_up, w_ffn_down)
    for l in range(DEPTH):
        x2, ctx2 = _layer(x2, ctx2, cv, tabs, l, sw, w_ada, b_ada, g_mix[l], sink_a[l], g_q_a[l], g_kv_a[l], rpb_c[l],
                          g_ffn[l], last=(l == DEPTH - 1))
    zero = jnp.zeros((1, 1, d), F32)
    out = _norm_mod(x2, g_final, zero, zero, seq, out_dtype=F32)
    return out.reshape(bsz, seq, d)
```

```python
import functools
import math

import numpy as np
import jax
import jax.numpy as jnp
from jax import lax
from jax.experimental import pallas as pl
from jax.experimental.pallas import tpu as pltpu

D_MODEL = 4096
BATCH = 2
SEQ = 8192
DEPTH = 2
GRID_W = 64
CTX_LEN = 256
HEAD_DIM = 128
A_HEADS = 16
A_KV_HEADS = 4
WINDOW = 128
BLOCK = 128
B_HEADS = 16
Q_LORA = 1024
KV_LORA = 512
NOPE_DIM = 128
ROPE_DIM = 64
V_DIM = 128
C_HEADS = 16
NA_ROWS = 8
NA_COLS = 16
N_BRANCH = 3
ROPE_BASE = 10000.0
EPS = 1e-6
NEG_INF = -1e30

LANE = 128
MLA_QK = 2 * LANE
LOG2E = math.log2(math.e)
MLA_LOGIT_SCALE = (NOPE_DIM + ROPE_DIM) ** -0.5 * LOG2E
HEAD_LOGIT_SCALE = HEAD_DIM ** -0.5 * LOG2E
NA_QROWS = 4
VMEM_LIMIT = 56 * 1024 * 1024

BF16 = jnp.bfloat16
F32 = jnp.float32


def _params(n_axes):
    return pltpu.CompilerParams(dimension_semantics=("parallel",) * n_axes,
                                vmem_limit_bytes=VMEM_LIMIT)


def _dot(a, b):
    return jnp.dot(a, b, preferred_element_type=F32)


def _dot_nt(a, b):
    return lax.dot_general(a, b, (((1,), (1,)), ((), ())), preferred_element_type=F32)


def _tile(n, pref):
    if n <= pref:
        return n
    t = pref - pref % LANE
    while t >= LANE:
        if n % t == 0:
            return t
        t -= LANE
    return n


def _adaln_body(c_ref, w_ref, b_ref, o_ref):
    cv = c_ref[...]
    o_ref[...] = _dot(cv * jax.nn.sigmoid(cv), w_ref[...]) + b_ref[...]


def _adaln(cv, w_ada, b_ada, layer):
    rows, d = cv.shape
    n = w_ada.shape[2]
    bn = _tile(n, 512)
    return pl.pallas_call(
        _adaln_body,
        grid=(n // bn,),
        in_specs=[pl.BlockSpec((rows, d), lambda j: (0, 0)),
                  pl.BlockSpec((None, d, bn), lambda j: (layer, 0, j)),
                  pl.BlockSpec((None, 1, bn), lambda j: (layer, 0, j))],
        out_specs=pl.BlockSpec((rows, bn), lambda j: (0, j)),
        out_shape=jax.ShapeDtypeStruct((rows, n), F32),
        compiler_params=_params(1),
        name="adaln",
    )(cv, w_ada, b_ada.reshape(b_ada.shape[0], 1, n))


def _norm_mod_body(x_ref, g_ref, sh_ref, sc_ref, o_ref):
    x = x_ref[...]
    y = x * lax.rsqrt(jnp.mean(x * x, axis=-1, keepdims=True) + EPS) * g_ref[...]
    o_ref[...] = (y * (1.0 + sc_ref[...]) + sh_ref[...]).astype(o_ref.dtype)


def _norm_mod(x2, g, shift, scale, rows_per_batch, out_dtype=BF16):
    m, d = x2.shape
    ts = _tile(rows_per_batch, 512)
    tpb = rows_per_batch // ts
    per_batch = shift.shape[0] > 1
    mod_spec = pl.BlockSpec((None, 1, d), (lambda i: (i // tpb, 0, 0)) if per_batch else (lambda i: (0, 0, 0)))
    return pl.pallas_call(
        _norm_mod_body,
        grid=(m // ts,),
        in_specs=[pl.BlockSpec((ts, d), lambda i: (i, 0)),
                  pl.BlockSpec((1, d), lambda i: (0, 0)),
                  mod_spec, mod_spec],
        out_specs=pl.BlockSpec((ts, d), lambda i: (i, 0)),
        out_shape=jax.ShapeDtypeStruct((m, d), out_dtype),
        compiler_params=_params(1),
        name="norm_mod",
    )(x2, g.reshape(1, d), shift, scale)


def _a_spec(bm, k):
    return pl.BlockSpec((bm, k), lambda i, j: (i, 0))


class _LayerWeight:
    def __init__(self, arr, layer, ncols=None):
        self.arr, self.layer = arr, layer
        self.shape = (arr.shape[1], arr.shape[2] if ncols is None else ncols)

    def cols(self, ncols):
        return _LayerWeight(self.arr, self.layer, ncols)


def _w_spec(w, bn):
    return pl.BlockSpec((None, w.shape[0], bn), lambda i, j: (w.layer, 0, j))


def _mn_spec(bm, bn):
    return pl.BlockSpec((bm, bn), lambda i, j: (i, j))


def _pos_spec(bm, width, tiles_per_batch):
    return pl.BlockSpec((bm, width), lambda i, j: (i % tiles_per_batch, 0))


def _mm_call(body, grid, in_specs, out_specs, out_shape, args, name):
    args = [a.arr if isinstance(a, _LayerWeight) else a for a in args]
    return pl.pallas_call(body, grid=grid, in_specs=in_specs, out_specs=out_specs, out_shape=out_shape,
                          compiler_params=_params(2), name=name)(*args)


def _plain_body(a_ref, w_ref, o_ref):
    o_ref[...] = _dot(a_ref[...], w_ref[...]).astype(o_ref.dtype)


def _scaled_body(a_ref, w_ref, cs_ref, o_ref):
    o_ref[...] = (_dot(a_ref[...], w_ref[...]) * cs_ref[...]).astype(o_ref.dtype)


def _n_spec(bn):
    return pl.BlockSpec((1, bn), lambda i, j: (0, j))


def _mm_plain(a, w, colscale=None, bm=2048, bn=512, name="mm_plain"):
    m, k = a.shape
    n = w.shape[1]
    bm, bn = _tile(m, bm), _tile(n, bn)
    if colscale is None:
        return _mm_call(_plain_body, (m // bm, n // bn), [_a_spec(bm, k), _w_spec(w, bn)], _mn_spec(bm, bn),
                        jax.ShapeDtypeStruct((m, n), BF16), (a, w), name)
    return _mm_call(_scaled_body, (m // bm, n // bn), [_a_spec(bm, k), _w_spec(w, bn), _n_spec(bn)],
                    _mn_spec(bm, bn), jax.ShapeDtypeStruct((m, n), BF16), (a, w, colscale), name)


def _rope_chunks(acc, cos, sin):
    outs = []
    for c in range(acc.shape[1] // LANE):
        t = acc[:, c * LANE:(c + 1) * LANE]
        outs.append(t * cos + pltpu.roll(t, LANE // 2, 1) * sin)
    return outs[0] if len(outs) == 1 else jnp.concatenate(outs, axis=1)


def _rope_body(a_ref, w_ref, cs_ref, cos_ref, sin_ref, o_ref):
    acc = _dot(a_ref[...], w_ref[...]) * cs_ref[...]
    o_ref[...] = _rope_chunks(acc, cos_ref[...], sin_ref[...]).astype(o_ref.dtype)


def _mm_rope(a, w, colscale, cos2, sin2, rows_per_batch, bm=2048, bn=512):
    m, k = a.shape
    n = w.shape[1]
    bm, bn = _tile(rows_per_batch, bm), _tile(n, bn)
    tpb = rows_per_batch // bm
    return _mm_call(_rope_body, (m // bm, n // bn),
                    [_a_spec(bm, k), _w_spec(w, bn), _n_spec(bn), _pos_spec(bm, LANE, tpb), _pos_spec(bm, LANE, tpb)],
                    _mn_spec(bm, bn), jax.ShapeDtypeStruct((m, n), BF16), (a, w, colscale, cos2, sin2), "mm_rope")


def _ckv_body(a_ref, w_ref, g_ref, *rest, rope):
    if rope:
        cos_ref, sin_ref, ckv_ref, kr_ref = rest
    else:
        ckv_ref, kr_ref = rest
    acc = _dot(a_ref[...], w_ref[...])
    nl = ckv_ref.shape[1]
    t = acc[:, :nl]
    ckv_ref[...] = (t * lax.rsqrt(jnp.mean(t * t, axis=-1, keepdims=True) + EPS) * g_ref[...]).astype(ckv_ref.dtype)
    r = acc[:, nl:]
    if rope:
        r = _rope_chunks(r, cos_ref[...], sin_ref[...])
    kr_ref[...] = r.astype(kr_ref.dtype)


def _mm_ckv(a, w, g, cos2, sin2, rows_per_batch, rope, bm=1024):
    m, k = a.shape
    n = w.shape[1]
    nl = n - LANE
    bm = _tile(rows_per_batch, bm)
    tpb = rows_per_batch // bm
    in_specs = [_a_spec(bm, k), _w_spec(w, n), pl.BlockSpec((1, nl), lambda i, j: (0, 0))]
    args = [a, w, g.reshape(1, nl)]
    if rope:
        in_specs += [_pos_spec(bm, LANE, tpb), _pos_spec(bm, LANE, tpb)]
        args += [cos2, sin2]
    return _mm_call(functools.partial(_ckv_body, rope=rope), (m // bm, 1), in_specs,
                    [pl.BlockSpec((bm, nl), lambda i, j: (i, 0)), pl.BlockSpec((bm, LANE), lambda i, j: (i, 0))],
                    [jax.ShapeDtypeStruct((m, nl), BF16), jax.ShapeDtypeStruct((m, LANE), BF16)], args, "mm_ckv")


def _rms_body(a_ref, w_ref, g_ref, o_ref):
    t = _dot(a_ref[...], w_ref[...])
    o_ref[...] = (t * lax.rsqrt(jnp.mean(t * t, axis=-1, keepdims=True) + EPS) * g_ref[...]).astype(o_ref.dtype)


def _mm_rms(a, w, g, bm=1024):
    m, k = a.shape
    n = w.shape[1]
    bm = _tile(m, bm)
    return _mm_call(_rms_body, (m // bm, 1), [_a_spec(bm, k), _w_spec(w, n), pl.BlockSpec((1, n), lambda i, j: (0, 0))],
                    _mn_spec(bm, n), jax.ShapeDtypeStruct((m, n), BF16), (a, w, g.reshape(1, n)), "mm_rms")


def _sigmoid_body(a_ref, w_ref, o_ref):
    o_ref[...] = jax.nn.sigmoid(_dot(a_ref[...], w_ref[...]))


def _mm_sigmoid(a, w, bm=1024, bn=1024):
    m, k = a.shape
    n = w.shape[1]
    bm, bn = _tile(m, bm), _tile(n, bn)
    return _mm_call(_sigmoid_body, (m // bm, n // bn), [_a_spec(bm, k), _w_spec(w, bn)], _mn_spec(bm, bn),
                    jax.ShapeDtypeStruct((m, n), F32), (a, w), "mm_gates")


def _qb_body(a_ref, w_ref, *rest, rope):
    if rope:
        cos_ref, sin_ref, o_ref = rest
    else:
        (o_ref,) = rest
    acc = _dot(a_ref[...], w_ref[...]) * MLA_LOGIT_SCALE
    if rope:
        parts = []
        for h in range(acc.shape[1] // MLA_QK):
            parts.append(acc[:, h * MLA_QK:h * MLA_QK + NOPE_DIM])
            parts.append(_rope_chunks(acc[:, h * MLA_QK + NOPE_DIM:(h + 1) * MLA_QK], cos_ref[...], sin_ref[...]))
        acc = jnp.concatenate(parts, axis=1)
    o_ref[...] = acc.astype(o_ref.dtype)


def _mm_qb(a, w, cos2, sin2, rows_per_batch, rope, bm=1024):
    m, k = a.shape
    n = w.shape[1]
    bm = _tile(rows_per_batch, bm)
    bn = 4 * MLA_QK if n % (4 * MLA_QK) == 0 else MLA_QK
    tpb = rows_per_batch // bm
    in_specs = [_a_spec(bm, k), _w_spec(w, bn)]
    args = [a, w]
    if rope:
        in_specs += [_pos_spec(bm, LANE, tpb), _pos_spec(bm, LANE, tpb)]
        args += [cos2, sin2]
    return _mm_call(functools.partial(_qb_body, rope=rope), (m // bm, n // bn), in_specs, _mn_spec(bm, bn),
                    jax.ShapeDtypeStruct((m, n), BF16), args, "mm_qb")


def _kexp_body(a_ref, w_ref, kr_ref, o_ref):
    acc = _dot(a_ref[...], w_ref[...]).astype(o_ref.dtype)
    kr = kr_ref[...]
    parts = []
    for h in range(acc.shape[1] // NOPE_DIM):
        parts += [acc[:, h * NOPE_DIM:(h + 1) * NOPE_DIM], kr]
    o_ref[...] = jnp.concatenate(parts, axis=1)


def _mm_kexp(ckv, w_uk, kr, bm=1536, heads_per_tile=8):
    m, k = ckv.shape
    n = w_uk.shape[1]
    heads = n // NOPE_DIM
    bm = _tile(m, bm)
    hpt = heads_per_tile if heads % heads_per_tile == 0 else 1
    return _mm_call(_kexp_body, (m // bm, heads // hpt),
                    [_a_spec(bm, k), _w_spec(w_uk, hpt * NOPE_DIM), pl.BlockSpec((bm, LANE), lambda i, j: (i, 0))],
                    _mn_spec(bm, hpt * MLA_QK), jax.ShapeDtypeStruct((m, heads * MLA_QK), BF16),
                    (ckv, w_uk, kr), "mm_kexp")


def _vexp_body(a_ref, w_ref, o_ref):
    acc = _dot(a_ref[...], w_ref[...]).astype(o_ref.dtype)
    lane = lax.broadcasted_iota(jnp.int32, (acc.shape[0], V_DIM), 1)
    ones = jnp.where(lane == 0, 1.0, 0.0).astype(o_ref.dtype)
    parts = []
    for h in range(acc.shape[1] // V_DIM):
        parts += [acc[:, h * V_DIM:(h + 1) * V_DIM], ones]
    o_ref[...] = jnp.concatenate(parts, axis=1)


def _mm_vexp(ckv, w_uv, bm=1536, heads_per_tile=8):
    m, k = ckv.shape
    heads = w_uv.shape[1] // V_DIM
    bm = _tile(m, bm)
    hpt = heads_per_tile if heads % heads_per_tile == 0 else 1
    return _mm_call(_vexp_body, (m // bm, heads // hpt), [_a_spec(bm, k), _w_spec(w_uv, hpt * V_DIM)],
                    _mn_spec(bm, 2 * hpt * V_DIM), jax.ShapeDtypeStruct((m, 2 * heads * V_DIM), BF16),
                    (ckv, w_uv), "mm_vexp")


def _merge_body(oa_ref, ob_ref, oc_ref, wa_ref, wb_ref, wc_ref, ga_ref, gb_ref, gc_ref, o_ref):
    y = ga_ref[...] * _dot(oa_ref[...], wa_ref[...])
    y = y + gb_ref[...] * _dot(ob_ref[...], wb_ref[...])
    y = y + gc_ref[...] * _dot(oc_ref[...], wc_ref[...])
    o_ref[...] = y.astype(o_ref.dtype)


def _mm_merge(o_a, o_b, o_c, w_br, layer, gates, bm=1024, bn=256):
    m, k = o_a.shape
    n = w_br.shape[3]
    bm, bn = _tile(m, bm), _tile(n, bn)
    nb = n // bn
    wspec = [pl.BlockSpec((None, None, k, bn), (lambda i, j, r=r: (layer, r, 0, j))) for r in range(N_BRANCH)]
    gspec = [pl.BlockSpec((bm, bn), (lambda i, j, r=r: (i, r * nb + j))) for r in range(N_BRANCH)]
    return _mm_call(_merge_body, (m // bm, nb), [_a_spec(bm, k)] * 3 + wspec + gspec, _mn_spec(bm, bn),
                    jax.ShapeDtypeStruct((m, n), BF16), (o_a, o_b, o_c, w_br, w_br, w_br, gates, gates, gates),
                    "mm_merge")


def _resid_body(a_ref, w_ref, x_ref, gt_ref, o_ref):
    o_ref[...] = x_ref[...] + gt_ref[...] * _dot(a_ref[...], w_ref[...])


def _mm_resid(a, w, x2, gate, rows_per_batch, bm=1024, bn=512, name="mm_resid"):
    m, k = a.shape
    n = w.shape[1]
    bm, bn = _tile(rows_per_batch, bm), _tile(n, bn)
    tpb = rows_per_batch // bm
    per_batch = gate.shape[0] > 1
    gspec = pl.BlockSpec((None, 1, bn), (lambda i, j: (i // tpb, 0, j)) if per_batch else (lambda i, j: (0, 0, j)))
    return _mm_call(_resid_body, (m // bm, n // bn), [_a_spec(bm, k), _w_spec(w, bn), _mn_spec(bm, bn), gspec],
                    _mn_spec(bm, bn), jax.ShapeDtypeStruct((m, n), F32), (a, w, x2, gate), name)


def _ffn_up_body(a_ref, wg_ref, wu_ref, o_ref):
    a = a_ref[...]
    g = _dot(a, wg_ref[...])
    o_ref[...] = (g * jax.nn.sigmoid(g) * _dot(a, wu_ref[...])).astype(o_ref.dtype)


def _mm_ffn_up(a, wg, wu, bm=2048, bn=256):
    m, k = a.shape
    n = wg.shape[1]
    bm, bn = _tile(m, bm), _tile(n, bn)
    return _mm_call(_ffn_up_body, (m // bm, n // bn), [_a_spec(bm, k), _w_spec(wg, bn), _w_spec(wu, bn)],
                    _mn_spec(bm, bn), jax.ShapeDtypeStruct((m, n), BF16), (a, wg, wu), "mm_ffn_up")


def _with_ones_col(v):
    lane = lax.broadcasted_iota(jnp.int32, v.shape, 1)
    return jnp.concatenate([v, jnp.where(lane == 0, 1.0, 0.0).astype(v.dtype)], axis=1)


def _win_body(sink_ref, q_ref, kp_ref, kc_ref, kn_ref, vp_ref, vc_ref, vn_ref, kx_ref, vx_ref, mask_ref, o_ref, *,
              groups):
    kvh = pl.program_id(1)
    k_all = jnp.concatenate([kp_ref[...], kc_ref[...], kn_ref[...], kx_ref[...]], axis=0)
    v_all = _with_ones_col(jnp.concatenate([vp_ref[...], vc_ref[...], vn_ref[...], vx_ref[...]], axis=0))
    mask = mask_ref[...]
    for g in range(groups):
        q = q_ref[:, g * HEAD_DIM:(g + 1) * HEAD_DIM]
        s = _dot_nt(q, k_all) + mask
        sink = sink_ref[kvh * groups + g] * LOG2E
        m = jnp.maximum(jnp.max(s, axis=1, keepdims=True), sink)
        acc = _dot(jnp.exp2(s - m).astype(BF16), v_all)
        l = acc[:, HEAD_DIM:HEAD_DIM + 1] + jnp.exp2(sink - m)
        o_ref[:, g * HEAD_DIM:(g + 1) * HEAD_DIM] = (acc[:, :HEAD_DIM] / l).astype(o_ref.dtype)


def _win_masks(ctx):
    qi = np.arange(BLOCK)[:, None]
    kj = np.arange(3 * BLOCK)[None, :]
    band = np.abs(kj - qi - BLOCK) <= WINDOW
    out = []
    for lo, hi in ((BLOCK, 3 * BLOCK), (0, 3 * BLOCK), (0, 2 * BLOCK)):
        ok = band & (kj >= lo) & (kj < hi)
        out.append(np.concatenate([np.where(ok, 0.0, NEG_INF), np.zeros((BLOCK, ctx))], axis=1))
    return np.stack(out).astype(np.float32)


def _dense_body(*refs, tk, scale, has_sink, ones_col):
    if has_sink:
        sink_ref, q_ref, k_ref, v_ref, o_ref = refs
    else:
        q_ref, k_ref, v_ref, o_ref = refs
    q = q_ref[...]
    lk = k_ref.shape[0]
    dv = o_ref.shape[1]
    exp = jnp.exp2 if scale is None else jnp.exp
    m = l = acc = None
    for c in range(lk // tk):
        s = _dot_nt(q, k_ref[c * tk:(c + 1) * tk, :])
        if scale is not None:
            s = s * scale
        mc = jnp.max(s, axis=1, keepdims=True)
        if c == 0:
            m_new = mc
            if has_sink:
                sink = sink_ref[pl.program_id(1)]
                m_new = jnp.maximum(m_new, sink)
        else:
            m_new = jnp.maximum(m, mc)
        p = exp(s - m_new)
        pv = _dot(p.astype(BF16), v_ref[c * tk:(c + 1) * tk, :])
        if not ones_col:
            ps = jnp.sum(p, axis=1, keepdims=True)
            if c == 0:
                l = ps + exp(sink - m_new) if has_sink else ps
            else:
                l = exp(m - m_new) * l + ps
        acc = pv if c == 0 else exp(m - m_new) * acc + pv
        m = m_new
    if ones_col:
        l = acc[:, dv:dv + 1]
        acc = acc[:, :dv]
    o_ref[...] = (acc / l).astype(o_ref.dtype)


def _dense_attention(q3, k3, v3, *, heads, groups, dk, dv, q_off, k_off, v_off, lk, tq, tk, scale, sink=None,
                     ones_col=False, name="dense_attn"):
    assert not (ones_col and sink is not None)
    bsz, lq, _ = q3.shape
    tq = _tile(lq, tq)
    dvw = 2 * dv if ones_col else dv
    in_specs = [pl.BlockSpec((None, tq, dk), lambda b, h, i: (b, i, q_off + h)),
                pl.BlockSpec((None, lk, dk), lambda b, h, i: (b, 0, k_off + h // groups)),
                pl.BlockSpec((None, lk, dvw), lambda b, h, i: (b, 0, v_off + h // groups))]
    args = [q3, k3, v3]
    if sink is not None:
        in_specs = [pl.BlockSpec(memory_space=pltpu.SMEM)] + in_specs
        args = [sink] + args
    return pl.pallas_call(
        functools.partial(_dense_body, tk=tk, scale=scale, has_sink=sink is not None, ones_col=ones_col),
        grid=(bsz, heads, lq // tq),
        in_specs=in_specs,
        out_specs=pl.BlockSpec((None, tq, dv), lambda b, h, i: (b, i, h)),
        out_shape=jax.ShapeDtypeStruct((bsz, lq, heads * dv), BF16),
        compiler_params=_params(3),
        name=name,
    )(*args)


def _na_body(q_ref, kp_ref, kc_ref, kn_ref, vp_ref, vc_ref, vn_ref, kx_ref, vx_ref, bias_ref, o_ref):
    for h in range(bias_ref.shape[0]):
        cols = slice(h * HEAD_DIM, (h + 1) * HEAD_DIM)
        k_all = jnp.concatenate([kp_ref[:, cols], kc_ref[:, cols], kn_ref[:, cols], kx_ref[:, cols]], axis=0)
        v_all = _with_ones_col(
            jnp.concatenate([vp_ref[:, cols], vc_ref[:, cols], vn_ref[:, cols], vx_ref[:, cols]], axis=0))
        s = _dot_nt(q_ref[:, cols], k_all) + bias_ref[h]
        m = jnp.max(s, axis=1, keepdims=True)
        acc = _dot(jnp.exp2(s - m).astype(BF16), v_all)
        o_ref[:, cols] = (acc[:, :HEAD_DIM] / acc[:, HEAD_DIM:HEAD_DIM + 1]).astype(o_ref.dtype)


def _na_valid(rows):
    r = NA_QROWS
    nblk = rows // r
    assert rows % r == 0 and nblk >= 3 and rows >= NA_ROWS and NA_ROWS >= 2 * r
    kwin = min(NA_ROWS, rows)
    valid = []
    for i in (0, 1, nblk - 1):
        qrow = (r * i + np.arange(r))[:, None, None, None]
        qcol = np.arange(GRID_W)[None, :, None, None]
        krow = (r * (i - 1) + np.arange(3 * r))[None, None, :, None]
        kcol = np.arange(GRID_W)[None, None, None, :]
        r0 = np.clip(qrow - kwin // 2, 0, rows - kwin)
        c0 = np.clip(qcol - NA_COLS // 2, 0, GRID_W - NA_COLS)
        ok = (krow >= 0) & (krow < rows) & (krow >= r0) & (krow < r0 + kwin) & (kcol >= c0) & (kcol < c0 + NA_COLS)
        valid.append(ok.reshape(r * GRID_W, 3 * r * GRID_W))
    return np.stack(valid)


def _na_bias_tables(rpb, rows, ctx):
    r, w = NA_QROWS, GRID_W
    heads, nro, nco = rpb.shape
    circ = jnp.concatenate([rpb[..., NA_COLS - 1:], jnp.zeros((heads, nro, 2 * w - nco), rpb.dtype),
                            rpb[..., :NA_COLS - 1]], axis=-1)
    toep = jnp.broadcast_to(circ[:, :, None, :], (heads, nro, w, 2 * w)).reshape(heads, nro, 2 * w * w)
    toep = toep[:, :, :w * (2 * w - 1)].reshape(heads, nro, w, 2 * w - 1)[..., :w]
    strips = []
    for qr in range(r):
        lo = NA_ROWS - 1 - r - qr
        strips.append(toep[:, lo:lo + 3 * r].transpose(0, 2, 1, 3).reshape(heads, w, 3 * r * w))
    tile = jnp.concatenate(strips, axis=1) * LOG2E
    local = jnp.where(jnp.asarray(_na_valid(rows))[None], tile[:, None], NEG_INF)
    return jnp.concatenate([local, jnp.zeros(local.shape[:3] + (ctx,), local.dtype)], axis=-1)


def _neighborhood_attention(plain, plain_c, rpb, bsz, seq, q_blk, k_blk, v_blk, heads_per_step):
    rows = seq // GRID_W
    tq = NA_QROWS * GRID_W
    nblk = rows // NA_QROWS
    ctx = plain_c.shape[0] // bsz
    hb = heads_per_step
    hw = hb * HEAD_DIM
    bias = _na_bias_tables(rpb, rows, ctx)

    def blk(off, d):
        def imap(b, h, i):
            return (b * nblk + jnp.clip(i + d, 0, nblk - 1), off + h)
        return pl.BlockSpec((tq, hw), imap)

    def bias_map(b, h, i):
        return (h, jnp.where(i == 0, 0, jnp.where(i == nblk - 1, 2, 1)), 0, 0)

    in_specs = [blk(q_blk, 0),
                blk(k_blk, -1), blk(k_blk, 0), blk(k_blk, 1),
                blk(v_blk, -1), blk(v_blk, 0), blk(v_blk, 1),
                pl.BlockSpec((ctx, hw), lambda b, h, i: (b, k_blk + h)),
                pl.BlockSpec((ctx, hw), lambda b, h, i: (b, v_blk + h)),
                pl.BlockSpec((hb, None, tq, 3 * tq + ctx), bias_map)]
    return pl.pallas_call(
        _na_body,
        grid=(bsz, C_HEADS // hb, nblk),
        in_specs=in_specs,
        out_specs=pl.BlockSpec((tq, hw), lambda b, h, i: (b * nblk + i, h)),
        out_shape=jax.ShapeDtypeStruct((bsz * seq, C_HEADS * HEAD_DIM), BF16),
        compiler_params=_params(3),
        name="na_attn",
    )(plain, plain, plain, plain, plain, plain, plain, plain_c, plain_c, bias)


def _rope_tables(seq):
    t = jnp.arange(seq, dtype=jnp.int32)
    row = (t // GRID_W).astype(F32)
    col = (t % GRID_W).astype(F32)

    def cs(rot_dim):
        n_freq = rot_dim // 4
        inv = jnp.power(ROPE_BASE, -jnp.arange(n_freq, dtype=F32) / n_freq)
        ang = jnp.concatenate([row[:, None] * inv, col[:, None] * inv], axis=-1)
        return jnp.cos(ang), jnp.sin(ang)

    ca, sa = cs(HEAD_DIM)
    cb, sb = cs(ROPE_DIM)
    zb = jnp.zeros_like(cb)
    return (jnp.concatenate([ca, ca], axis=1), jnp.concatenate([-sa, sa], axis=1),
            jnp.concatenate([cb, zb, cb, zb], axis=1), jnp.concatenate([-sb, zb, sb, zb], axis=1))


def _spread_rope_cols(w):
    h = ROPE_DIM // 2
    z = jnp.zeros(w.shape[:-1] + (LANE // 2 - h,), w.dtype)
    return jnp.concatenate([w[..., :h], z, w[..., h:], z], axis=-1)


def _stacked_weights(w_in, w_q_b, w_kv_b, w_br, w_out, w_ffn_gate, w_ffn_up, w_ffn_down):
    nl = w_in.shape[0]
    kvh = A_KV_HEADS * HEAD_DIM
    ch = C_HEADS * HEAD_DIM
    ah = A_HEADS * HEAD_DIM
    sizes = [kvh, kvh, KV_LORA, ROPE_DIM, ch, ch, ah, Q_LORA, ch, N_BRANCH * D_MODEL]
    offs = np.concatenate([[0], np.cumsum(sizes)])
    a_k, a_v, b_ckv, b_kr, c_k, c_v, a_q, b_qa, c_q, gate = [w_in[..., offs[i]:offs[i + 1]]
                                                             for i in range(len(sizes))]
    return {
        "rope": jnp.concatenate([a_k, a_q], axis=-1).astype(BF16),
        "plain": jnp.concatenate([a_v, c_k, c_v, c_q], axis=-1).astype(BF16),
        "ckv": jnp.concatenate([b_ckv, _spread_rope_cols(b_kr)], axis=-1).astype(BF16),
        "bqa": b_qa.astype(BF16),
        "gate": gate.astype(BF16),
        "qb": jnp.concatenate([w_q_b[..., :NOPE_DIM], _spread_rope_cols(w_q_b[..., NOPE_DIM:])], axis=-1)
        .reshape(nl, Q_LORA, B_HEADS * MLA_QK).astype(BF16),
        "uk": w_kv_b[..., :NOPE_DIM].reshape(nl, KV_LORA, B_HEADS * NOPE_DIM).astype(BF16),
        "uv": w_kv_b[..., NOPE_DIM:].reshape(nl, KV_LORA, B_HEADS * V_DIM).astype(BF16),
        "br": w_br.astype(BF16),
        "out": w_out.astype(BF16),
        "ffn_gate": w_ffn_gate.astype(BF16),
        "ffn_up": w_ffn_up.astype(BF16),
        "ffn_down": w_ffn_down.astype(BF16),
    }


def _layer(x2, ctx2, cv, tabs, layer, sw, w_ada, b_ada, g_mix, sink_a, g_q_a, g_kv_a, rpb_c, g_ffn, last):
    bsz, seq, ctx, d = BATCH, SEQ, CTX_LEN, D_MODEL
    cos_a, sin_a, cos_b, sin_b = tabs
    kvb = A_KV_HEADS * HEAD_DIM // LANE
    chb = C_HEADS * HEAD_DIM // LANE
    na_hb = math.gcd(math.gcd(kvb, chb), 4)

    mod = _adaln(cv, w_ada, b_ada, layer)
    sh_m, sc_m, gt_m, sh_f, sc_f, gt_f = [mod[:bsz, i * d:(i + 1) * d].reshape(bsz, 1, d) for i in range(6)]
    csh_m, csc_m, cgt_m, csh_f, csc_f, cgt_f = [mod[bsz:bsz + 1, i * d:(i + 1) * d].reshape(1, 1, d) for i in range(6)]
    w = {name: _LayerWeight(arr, layer) for name, arr in sw.items() if name != "br"}

    cm = bsz * ctx
    hc = _norm_mod(ctx2, g_mix, csh_m, csc_m, cm)
    ropecols_c = _mm_plain(hc, w["rope"] if not last else w["rope"].cols(kvb * LANE), bm=cm, name="mm_plain_c")
    plain_c = _mm_plain(hc, w["plain"] if not last else w["plain"].cols((kvb + 2 * chb) * LANE), bm=cm,
                        name="mm_plain_c")
    ckv_c, kr_c = _mm_ckv(hc, w["ckv"], g_kv_a, None, None, cm, rope=False, bm=cm)

    hx = _norm_mod(x2, g_mix, sh_m, sc_m, seq)
    qscale = jnp.full((1, A_HEADS * HEAD_DIM), HEAD_LOGIT_SCALE, F32)
    cs_rope = jnp.concatenate([jnp.ones((1, kvb * LANE), F32), qscale], axis=1)
    cs_plain = jnp.concatenate([jnp.ones((1, (kvb + 2 * chb) * LANE), F32),
                                jnp.full((1, chb * LANE), HEAD_LOGIT_SCALE, F32)], axis=1)
    qk_a = _mm_rope(hx, w["rope"], cs_rope, cos_a, sin_a, seq)
    plain = _mm_plain(hx, w["plain"], cs_plain)
    ckv, kr = _mm_ckv(hx, w["ckv"], g_kv_a, cos_b, sin_b, seq, rope=True)
    bqa = _mm_rms(hx, w["bqa"], g_q_a)
    gates = _mm_sigmoid(hx, w["gate"])
    qb = _mm_qb(bqa, w["qb"], cos_b, sin_b, seq, rope=True)

    lk = ctx + seq
    ckv_all = jnp.concatenate([ckv_c.reshape(bsz, ctx, -1), ckv.reshape(bsz, seq, -1)], axis=1).reshape(bsz * lk, -1)
    kr_all = jnp.concatenate([kr_c.reshape(bsz, ctx, -1), kr.reshape(bsz, seq, -1)], axis=1).reshape(bsz * lk, -1)
    k_exp = _mm_kexp(ckv_all, w["uk"], kr_all).reshape(bsz, lk, -1)
    v_exp = _mm_vexp(ckv_all, w["uv"]).reshape(bsz, lk, -1)

    o_a = _window_attention(sink_a, qk_a, plain, ropecols_c, plain_c, bsz, seq, kvb)
    o_b = _dense_attention(qb.reshape(bsz, seq, -1), k_exp, v_exp, heads=B_HEADS, groups=1, dk=MLA_QK, dv=V_DIM,
                           q_off=0, k_off=0, v_off=0, lk=lk, tq=1024, tk=_mla_chunk(lk), scale=None, ones_col=True,
                           name="mla_attn").reshape(bsz * seq, -1)
    o_c = _neighborhood_attention(plain, plain_c, rpb_c, bsz, seq, q_blk=(kvb + 2 * chb) // na_hb, k_blk=kvb // na_hb,
                                  v_blk=(kvb + chb) // na_hb, heads_per_step=na_hb)

    y = _mm_merge(o_a, o_b, o_c, sw["br"], layer, gates)
    x2 = _mm_resid(y, w["out"], x2, gt_m, seq)
    h2 = _norm_mod(x2, g_ffn, sh_f, sc_f, seq)
    act = _mm_ffn_up(h2, w["ffn_gate"], w["ffn_up"])
    x2 = _mm_resid(act, w["ffn_down"], x2, gt_f, seq, bm=512, bn=512, name="mm_ffn_down")

    if not last:
        gates_c = _mm_sigmoid(hc, w["gate"], bm=cm)
        bqa_c = _mm_rms(hc, w["bqa"], g_q_a, bm=cm)
        qb_c = _mm_qb(bqa_c, w["qb"], None, None, cm, rope=False, bm=cm)
        rc3 = ropecols_c.reshape(bsz, ctx, -1)
        pc3 = plain_c.reshape(bsz, ctx, -1)
        oc_a = _dense_attention(rc3, rc3, pc3, heads=A_HEADS, groups=A_HEADS // A_KV_HEADS, dk=HEAD_DIM, dv=HEAD_DIM,
                                q_off=kvb, k_off=0, v_off=0, lk=ctx, tq=ctx, tk=ctx, scale=HEAD_DIM ** -0.5,
                                sink=sink_a, name="ctx_attn_a").reshape(bsz * ctx, -1)
        oc_b = _dense_attention(qb_c.reshape(bsz, ctx, -1), k_exp, v_exp, heads=B_HEADS, groups=1, dk=MLA_QK,
                                dv=V_DIM, q_off=0, k_off=0, v_off=0, lk=ctx, tq=ctx, tk=ctx, scale=None, ones_col=True,
                                name="ctx_attn_b").reshape(bsz * ctx, -1)
        oc_c = _dense_attention(pc3, pc3, pc3, heads=C_HEADS, groups=1, dk=HEAD_DIM, dv=HEAD_DIM,
                                q_off=kvb + 2 * chb, k_off=kvb, v_off=kvb + chb, lk=ctx, tq=ctx, tk=ctx,
                                scale=HEAD_DIM ** -0.5, name="ctx_attn_c").reshape(bsz * ctx, -1)
        yc = _mm_merge(oc_a, oc_b, oc_c, sw["br"], layer, gates_c, bm=cm, bn=512)
        ctx2 = _mm_resid(yc, w["out"], ctx2, cgt_m, cm, bm=cm, name="mm_resid_c")
        hc2 = _norm_mod(ctx2, g_ffn, csh_f, csc_f, cm)
        act_c = _mm_ffn_up(hc2, w["ffn_gate"], w["ffn_up"], bm=cm)
        ctx2 = _mm_resid(act_c, w["ffn_down"], ctx2, cgt_f, cm, bm=cm, bn=256, name="mm_ffn_down_c")
    return x2, ctx2


def _mla_chunk(lk):
    for parts in range(40, 0, -1):
        if lk % parts == 0 and (lk // parts) % (2 * LANE) == 0:
            return lk // parts
    return lk


def _window_attention(sink, qk_a, plain, ropecols_c, plain_c, bsz, seq, kvb):
    nblk = seq // BLOCK
    groups = A_HEADS // A_KV_HEADS
    gw = groups * HEAD_DIM
    ctx = plain_c.shape[0] // bsz

    def blk(d):
        def imap(b, h, n):
            return (b * nblk + jnp.clip(n + d, 0, nblk - 1), h)
        return pl.BlockSpec((BLOCK, HEAD_DIM), imap)

    cspec = pl.BlockSpec((ctx, HEAD_DIM), lambda b, h, n: (b, h))
    mspec = pl.BlockSpec((None, BLOCK, 3 * BLOCK + ctx),
                         lambda b, h, n: (jnp.where(n == 0, 0, jnp.where(n == nblk - 1, 2, 1)), 0, 0))
    in_specs = [pl.BlockSpec(memory_space=pltpu.SMEM),
                pl.BlockSpec((BLOCK, gw), lambda b, h, n: (b * nblk + n, kvb // groups + h)),
                blk(-1), blk(0), blk(1), blk(-1), blk(0), blk(1), cspec, cspec, mspec]
    return pl.pallas_call(
        functools.partial(_win_body, groups=groups),
        grid=(bsz, A_KV_HEADS, nblk),
        in_specs=in_specs,
        out_specs=pl.BlockSpec((BLOCK, gw), lambda b, h, n: (b * nblk + n, h)),
        out_shape=jax.ShapeDtypeStruct((bsz * seq, A_HEADS * HEAD_DIM), BF16),
        compiler_params=_params(3),
        name="win_attn",
    )(sink, qk_a, qk_a, qk_a, qk_a, plain, plain, plain, ropecols_c, plain_c, jnp.asarray(_win_masks(ctx)))


def kernel(x, c, ctx, c_ctx, w_ada, b_ada, g_mix, w_in, sink_a, g_q_a, w_q_b, g_kv_a, w_kv_b, rpb_c, w_br, w_out,
           g_ffn, w_ffn_gate, w_ffn_up, w_ffn_down, g_final):
    bsz, seq, d = x.shape
    tabs = _rope_tables(seq)
    pad = (-(bsz + 1)) % 8
    cv = jnp.concatenate([c, c_ctx[None, :], jnp.zeros((pad, d), F32)], axis=0)
    x2 = x.reshape(bsz * seq, d)
    ctx2 = ctx.reshape(bsz * ctx.shape[1], d)
    sw = _stacked_weights(w_in, w_q_b, w_kv_b, w_br, w_out, w_ffn_gate, w_ffn_up, w_ffn_down)
    for l in range(DEPTH):
        x2, ctx2 = _layer(x2, ctx2, cv, tabs, l, sw, w_ada, b_ada, g_mix[l], sink_a[l], g_q_a[l], g_kv_a[l], rpb_c[l],
                          g_ffn[l], last=(l == DEPTH - 1))
    zero = jnp.zeros((1, 1, d), F32)
    out = _norm_mod(x2, g_final, zero, zero, seq, out_dtype=F32)
    return out.reshape(bsz, seq, d)
```

```python
import functools
import math

import numpy as np
import jax
import jax.numpy as jnp
from jax import lax
from jax.experimental import pallas as pl
from jax.experimental.pallas import tpu as pltpu

D_MODEL = 4096
BATCH = 2
SEQ = 8192
DEPTH = 2
GRID_W = 64
CTX_LEN = 256
HEAD_DIM = 128
A_HEADS = 16
A_KV_HEADS = 4
WINDOW = 128
BLOCK = 128
B_HEADS = 16
Q_LORA = 1024
KV_LORA = 512
NOPE_DIM = 128
ROPE_DIM = 64
V_DIM = 128
C_HEADS = 16
NA_ROWS = 8
NA_COLS = 16
N_BRANCH = 3
ROPE_BASE = 10000.0
EPS = 1e-6
NEG_INF = -1e30

LANE = 128
MLA_QK = 2 * LANE
LOG2E = math.log2(math.e)
MLA_LOGIT_SCALE = (NOPE_DIM + ROPE_DIM) ** -0.5 * LOG2E
HEAD_LOGIT_SCALE = HEAD_DIM ** -0.5 * LOG2E
NA_QROWS = 4
VMEM_LIMIT = 56 * 1024 * 1024

BF16 = jnp.bfloat16
F32 = jnp.float32


def _params(n_axes):
    return pltpu.CompilerParams(dimension_semantics=("parallel",) * n_axes,
                                vmem_limit_bytes=VMEM_LIMIT)


def _dot(a, b):
    return jnp.dot(a, b, preferred_element_type=F32)


def _dot_nt(a, b):
    return lax.dot_general(a, b, (((1,), (1,)), ((), ())), preferred_element_type=F32)


def _tile(n, pref):
    if n <= pref:
        return n
    t = pref - pref % LANE
    while t >= LANE:
        if n % t == 0:
            return t
        t -= LANE
    return n


def _adaln_body(c_ref, w_ref, b_ref, o_ref):
    cv = c_ref[...]
    o_ref[...] = _dot(cv * jax.nn.sigmoid(cv), w_ref[...]) + b_ref[...]


def _adaln(cv, w_ada, b_ada, layer):
    rows, d = cv.shape
    n = w_ada.shape[2]
    bn = _tile(n, 512)
    return pl.pallas_call(
        _adaln_body,
        grid=(n // bn,),
        in_specs=[pl.BlockSpec((rows, d), lambda j: (0, 0)),
                  pl.BlockSpec((None, d, bn), lambda j: (layer, 0, j)),
                  pl.BlockSpec((None, 1, bn), lambda j: (layer, 0, j))],
        out_specs=pl.BlockSpec((rows, bn), lambda j: (0, j)),
        out_shape=jax.ShapeDtypeStruct((rows, n), F32),
        compiler_params=_params(1),
        name="adaln",
    )(cv, w_ada, b_ada.reshape(b_ada.shape[0], 1, n))


def _norm_mod_body(x_ref, g_ref, sh_ref, sc_ref, o_ref):
    x = x_ref[...]
    y = x * lax.rsqrt(jnp.mean(x * x, axis=-1, keepdims=True) + EPS) * g_ref[...]
    o_ref[...] = (y * (1.0 + sc_ref[...]) + sh_ref[...]).astype(o_ref.dtype)


def _norm_mod(x2, g, shift, scale, rows_per_batch, out_dtype=BF16):
    m, d = x2.shape
    ts = _tile(rows_per_batch, 512)
    tpb = rows_per_batch // ts
    per_batch = shift.shape[0] > 1
    mod_spec = pl.BlockSpec((None, 1, d), (lambda i: (i // tpb, 0, 0)) if per_batch else (lambda i: (0, 0, 0)))
    return pl.pallas_call(
        _norm_mod_body,
        grid=(m // ts,),
        in_specs=[pl.BlockSpec((ts, d), lambda i: (i, 0)),
                  pl.BlockSpec((1, d), lambda i: (0, 0)),
                  mod_spec, mod_spec],
        out_specs=pl.BlockSpec((ts, d), lambda i: (i, 0)),
        out_shape=jax.ShapeDtypeStruct((m, d), out_dtype),
        compiler_params=_params(1),
        name="norm_mod",
    )(x2, g.reshape(1, d), shift, scale)


def _a_spec(bm, k):
    return pl.BlockSpec((bm, k), lambda i, j: (i, 0))


class _LayerWeight:
    def __init__(self, arr, layer, ncols=None):
        self.arr, self.layer = arr, layer
        self.shape = (arr.shape[1], arr.shape[2] if ncols is None else ncols)

    def cols(self, ncols):
        return _LayerWeight(self.arr, self.layer, ncols)


def _w_spec(w, bn):
    return pl.BlockSpec((None, w.shape[0], bn), lambda i, j: (w.layer, 0, j))


def _mn_spec(bm, bn):
    return pl.BlockSpec((bm, bn), lambda i, j: (i, j))


def _pos_spec(bm, width, tiles_per_batch):
    return pl.BlockSpec((bm, width), lambda i, j: (i % tiles_per_batch, 0))


def _mm_call(body, grid, in_specs, out_specs, out_shape, args, name):
    args = [a.arr if isinstance(a, _LayerWeight) else a for a in args]
    return pl.pallas_call(body, grid=grid, in_specs=in_specs, out_specs=out_specs, out_shape=out_shape,
                          compiler_params=_params(2), name=name)(*args)


def _plain_body(a_ref, w_ref, o_ref):
    o_ref[...] = _dot(a_ref[...], w_ref[...]).astype(o_ref.dtype)


def _scaled_body(a_ref, w_ref, cs_ref, o_ref):
    o_ref[...] = (_dot(a_ref[...], w_ref[...]) * cs_ref[...]).astype(o_ref.dtype)


def _n_spec(bn):
    return pl.BlockSpec((1, bn), lambda i, j: (0, j))


def _mm_plain(a, w, colscale=None, bm=2048, bn=512, name="mm_plain"):
    m, k = a.shape
    n = w.shape[1]
    bm, bn = _tile(m, bm), _tile(n, bn)
    if colscale is None:
        return _mm_call(_plain_body, (m // bm, n // bn), [_a_spec(bm, k), _w_spec(w, bn)], _mn_spec(bm, bn),
                        jax.ShapeDtypeStruct((m, n), BF16), (a, w), name)
    return _mm_call(_scaled_body, (m // bm, n // bn), [_a_spec(bm, k), _w_spec(w, bn), _n_spec(bn)],
                    _mn_spec(bm, bn), jax.ShapeDtypeStruct((m, n), BF16), (a, w, colscale), name)


def _rope_chunks(acc, cos, sin):
    outs = []
    for c in range(acc.shape[1] // LANE):
        t = acc[:, c * LANE:(c + 1) * LANE]
        outs.append(t * cos + pltpu.roll(t, LANE // 2, 1) * sin)
    return outs[0] if len(outs) == 1 else jnp.concatenate(outs, axis=1)


def _rope_body(a_ref, w_ref, cs_ref, cos_ref, sin_ref, o_ref):
    acc = _dot(a_ref[...], w_ref[...]) * cs_ref[...]
    o_ref[...] = _rope_chunks(acc, cos_ref[...], sin_ref[...]).astype(o_ref.dtype)


def _mm_rope(a, w, colscale, cos2, sin2, rows_per_batch, bm=2048, bn=512):
    m, k = a.shape
    n = w.shape[1]
    bm, bn = _tile(rows_per_batch, bm), _tile(n, bn)
    tpb = rows_per_batch // bm
    return _mm_call(_rope_body, (m // bm, n // bn),
                    [_a_spec(bm, k), _w_spec(w, bn), _n_spec(bn), _pos_spec(bm, LANE, tpb), _pos_spec(bm, LANE, tpb)],
                    _mn_spec(bm, bn), jax.ShapeDtypeStruct((m, n), BF16), (a, w, colscale, cos2, sin2), "mm_rope")


def _ckv_body(a_ref, w_ref, g_ref, *rest, rope):
    if rope:
        cos_ref, sin_ref, ckv_ref, kr_ref = rest
    else:
        ckv_ref, kr_ref = rest
    acc = _dot(a_ref[...], w_ref[...])
    nl = ckv_ref.shape[1]
    t = acc[:, :nl]
    ckv_ref[...] = (t * lax.rsqrt(jnp.mean(t * t, axis=-1, keepdims=True) + EPS) * g_ref[...]).astype(ckv_ref.dtype)
    r = acc[:, nl:]
    if rope:
        r = _rope_chunks(r, cos_ref[...], sin_ref[...])
    kr_ref[...] = r.astype(kr_ref.dtype)


def _mm_ckv(a, w, g, cos2, sin2, rows_per_batch, rope, bm=1024):
    m, k = a.shape
    n = w.shape[1]
    nl = n - LANE
    bm = _tile(rows_per_batch, bm)
    tpb = rows_per_batch // bm
    in_specs = [_a_spec(bm, k), _w_spec(w, n), pl.BlockSpec((1, nl), lambda i, j: (0, 0))]
    args = [a, w, g.reshape(1, nl)]
    if rope:
        in_specs += [_pos_spec(bm, LANE, tpb), _pos_spec(bm, LANE, tpb)]
        args += [cos2, sin2]
    return _mm_call(functools.partial(_ckv_body, rope=rope), (m // bm, 1), in_specs,
                    [pl.BlockSpec((bm, nl), lambda i, j: (i, 0)), pl.BlockSpec((bm, LANE), lambda i, j: (i, 0))],
                    [jax.ShapeDtypeStruct((m, nl), BF16), jax.ShapeDtypeStruct((m, LANE), BF16)], args, "mm_ckv")


def _rms_body(a_ref, w_ref, g_ref, o_ref):
    t = _dot(a_ref[...], w_ref[...])
    o_ref[...] = (t * lax.rsqrt(jnp.mean(t * t, axis=-1, keepdims=True) + EPS) * g_ref[...]).astype(o_ref.dtype)


def _mm_rms(a, w, g, bm=1024):
    m, k = a.shape
    n = w.shape[1]
    bm = _tile(m, bm)
    return _mm_call(_rms_body, (m // bm, 1), [_a_spec(bm, k), _w_spec(w, n), pl.BlockSpec((1, n), lambda i, j: (0, 0))],
                    _mn_spec(bm, n), jax.ShapeDtypeStruct((m, n), BF16), (a, w, g.reshape(1, n)), "mm_rms")


def _sigmoid_body(a_ref, w_ref, o_ref):
    o_ref[...] = jax.nn.sigmoid(_dot(a_ref[...], w_ref[...]))


def _mm_sigmoid(a, w, bm=1024, bn=1024):
    m, k = a.shape
    n = w.shape[1]
    bm, bn = _tile(m, bm), _tile(n, bn)
    return _mm_call(_sigmoid_body, (m // bm, n // bn), [_a_spec(bm, k), _w_spec(w, bn)], _mn_spec(bm, bn),
                    jax.ShapeDtypeStruct((m, n), F32), (a, w), "mm_gates")


def _qb_body(a_ref, w_ref, *rest, rope):
    if rope:
        cos_ref, sin_ref, o_ref = rest
    else:
        (o_ref,) = rest
    acc = _dot(a_ref[...], w_ref[...]) * MLA_LOGIT_SCALE
    if rope:
        parts = []
        for h in range(acc.shape[1] // MLA_QK):
            parts.append(acc[:, h * MLA_QK:h * MLA_QK + NOPE_DIM])
            parts.append(_rope_chunks(acc[:, h * MLA_QK + NOPE_DIM:(h + 1) * MLA_QK], cos_ref[...], sin_ref[...]))
        acc = jnp.concatenate(parts, axis=1)
    o_ref[...] = acc.astype(o_ref.dtype)


def _mm_qb(a, w, cos2, sin2, rows_per_batch, rope, bm=1024):
    m, k = a.shape
    n = w.shape[1]
    bm = _tile(rows_per_batch, bm)
    bn = 4 * MLA_QK if n % (4 * MLA_QK) == 0 else MLA_QK
    tpb = rows_per_batch // bm
    in_specs = [_a_spec(bm, k), _w_spec(w, bn)]
    args = [a, w]
    if rope:
        in_specs += [_pos_spec(bm, LANE, tpb), _pos_spec(bm, LANE, tpb)]
        args += [cos2, sin2]
    return _mm_call(functools.partial(_qb_body, rope=rope), (m // bm, n // bn), in_specs, _mn_spec(bm, bn),
                    jax.ShapeDtypeStruct((m, n), BF16), args, "mm_qb")


def _kexp_body(a_ref, w_ref, kr_ref, o_ref):
    acc = _dot(a_ref[...], w_ref[...]).astype(o_ref.dtype)
    kr = kr_ref[...]
    parts = []
    for h in range(acc.shape[1] // NOPE_DIM):
        parts += [acc[:, h * NOPE_DIM:(h + 1) * NOPE_DIM], kr]
    o_ref[...] = jnp.concatenate(parts, axis=1)


def _mm_kexp(ckv, w_uk, kr, bm=1536, heads_per_tile=8):
    m, k = ckv.shape
    n = w_uk.shape[1]
    heads = n // NOPE_DIM
    bm = _tile(m, bm)
    hpt = heads_per_tile if heads % heads_per_tile == 0 else 1
    return _mm_call(_kexp_body, (m // bm, heads // hpt),
                    [_a_spec(bm, k), _w_spec(w_uk, hpt * NOPE_DIM), pl.BlockSpec((bm, LANE), lambda i, j: (i, 0))],
                    _mn_spec(bm, hpt * MLA_QK), jax.ShapeDtypeStruct((m, heads * MLA_QK), BF16),
                    (ckv, w_uk, kr), "mm_kexp")


def _vexp_body(a_ref, w_ref, o_ref):
    acc = _dot(a_ref[...], w_ref[...]).astype(o_ref.dtype)
    lane = lax.broadcasted_iota(jnp.int32, (acc.shape[0], V_DIM), 1)
    ones = jnp.where(lane == 0, 1.0, 0.0).astype(o_ref.dtype)
    parts = []
    for h in range(acc.shape[1] // V_DIM):
        parts += [acc[:, h * V_DIM:(h + 1) * V_DIM], ones]
    o_ref[...] = jnp.concatenate(parts, axis=1)


def _mm_vexp(ckv, w_uv, bm=1536, heads_per_tile=8):
    m, k = ckv.shape
    heads = w_uv.shape[1] // V_DIM
    bm = _tile(m, bm)
    hpt = heads_per_tile if heads % heads_per_tile == 0 else 1
    return _mm_call(_vexp_body, (m // bm, heads // hpt), [_a_spec(bm, k), _w_spec(w_uv, hpt * V_DIM)],
                    _mn_spec(bm, 2 * hpt * V_DIM), jax.ShapeDtypeStruct((m, 2 * heads * V_DIM), BF16),
                    (ckv, w_uv), "mm_vexp")


def _merge_body(oa_ref, ob_ref, oc_ref, wa_ref, wb_ref, wc_ref, ga_ref, gb_ref, gc_ref, o_ref):
    y = ga_ref[...] * _dot(oa_ref[...], wa_ref[...])
    y = y + gb_ref[...] * _dot(ob_ref[...], wb_ref[...])
    y = y + gc_ref[...] * _dot(oc_ref[...], wc_ref[...])
    o_ref[...] = y.astype(o_ref.dtype)


def _mm_merge(o_a, o_b, o_c, w_br, layer, gates, bm=1024, bn=256):
    m, k = o_a.shape
    n = w_br.shape[3]
    bm, bn = _tile(m, bm), _tile(n, bn)
    nb = n // bn
    wspec = [pl.BlockSpec((None, None, k, bn), (lambda i, j, r=r: (layer, r, 0, j))) for r in range(N_BRANCH)]
    gspec = [pl.BlockSpec((bm, bn), (lambda i, j, r=r: (i, r * nb + j))) for r in range(N_BRANCH)]
    return _mm_call(_merge_body, (m // bm, nb), [_a_spec(bm, k)] * 3 + wspec + gspec, _mn_spec(bm, bn),
                    jax.ShapeDtypeStruct((m, n), BF16), (o_a, o_b, o_c, w_br, w_br, w_br, gates, gates, gates),
                    "mm_merge")


def _resid_body(a_ref, w_ref, x_ref, gt_ref, o_ref):
    o_ref[...] = x_ref[...] + gt_ref[...] * _dot(a_ref[...], w_ref[...])


def _mm_resid(a, w, x2, gate, rows_per_batch, bm=1024, bn=512, name="mm_resid"):
    m, k = a.shape
    n = w.shape[1]
    bm, bn = _tile(rows_per_batch, bm), _tile(n, bn)
    tpb = rows_per_batch // bm
    per_batch = gate.shape[0] > 1
    gspec = pl.BlockSpec((None, 1, bn), (lambda i, j: (i // tpb, 0, j)) if per_batch else (lambda i, j: (0, 0, j)))
    return _mm_call(_resid_body, (m // bm, n // bn), [_a_spec(bm, k), _w_spec(w, bn), _mn_spec(bm, bn), gspec],
                    _mn_spec(bm, bn), jax.ShapeDtypeStruct((m, n), F32), (a, w, x2, gate), name)


def _ffn_up_body(a_ref, wg_ref, wu_ref, o_ref):
    a = a_ref[...]
    g = _dot(a, wg_ref[...])
    o_ref[...] = (g * jax.nn.sigmoid(g) * _dot(a, wu_ref[...])).astype(o_ref.dtype)


def _mm_ffn_up(a, wg, wu, bm=2048, bn=256):
    m, k = a.shape
    n = wg.shape[1]
    bm, bn = _tile(m, bm), _tile(n, bn)
    return _mm_call(_ffn_up_body, (m // bm, n // bn), [_a_spec(bm, k), _w_spec(wg, bn), _w_spec(wu, bn)],
                    _mn_spec(bm, bn), jax.ShapeDtypeStruct((m, n), BF16), (a, wg, wu), "mm_ffn_up")


def _with_ones_col(v):
    lane = lax.broadcasted_iota(jnp.int32, v.shape, 1)
    return jnp.concatenate([v, jnp.where(lane == 0, 1.0, 0.0).astype(v.dtype)], axis=1)


def _win_body(sink_ref, q_ref, kp_ref, kc_ref, kn_ref, vp_ref, vc_ref, vn_ref, kx_ref, vx_ref, mask_ref, o_ref, *,
              groups):
    kvh = pl.program_id(1)
    k_all = jnp.concatenate([kp_ref[...], kc_ref[...], kn_ref[...], kx_ref[...]], axis=0)
    v_all = _with_ones_col(jnp.concatenate([vp_ref[...], vc_ref[...], vn_ref[...], vx_ref[...]], axis=0))
    mask = mask_ref[...]
    cols = [slice(g * HEAD_DIM, (g + 1) * HEAD_DIM) for g in range(groups)]
    s = [_dot_nt(q_ref[:, c], k_all) + mask for c in cols]
    sink = [sink_ref[kvh * groups + g] * LOG2E for g in range(groups)]
    m = [jnp.maximum(jnp.max(s[g], axis=1, keepdims=True), sink[g]) for g in range(groups)]
    p = [jnp.exp2(s[g] - m[g]).astype(BF16) for g in range(groups)]
    for g, c in enumerate(cols):
        acc = _dot(p[g], v_all)
        l = acc[:, HEAD_DIM:HEAD_DIM + 1] + jnp.exp2(sink[g] - m[g])
        o_ref[:, c] = (acc[:, :HEAD_DIM] / l).astype(o_ref.dtype)


def _win_masks(ctx):
    qi = np.arange(BLOCK)[:, None]
    kj = np.arange(3 * BLOCK)[None, :]
    band = np.abs(kj - qi - BLOCK) <= WINDOW
    out = []
    for lo, hi in ((BLOCK, 3 * BLOCK), (0, 3 * BLOCK), (0, 2 * BLOCK)):
        ok = band & (kj >= lo) & (kj < hi)
        out.append(np.concatenate([np.where(ok, 0.0, NEG_INF), np.zeros((BLOCK, ctx))], axis=1))
    return np.stack(out).astype(np.float32)


def _dense_body(*refs, tk, scale, has_sink, ones_col):
    if has_sink:
        sink_ref, q_ref, k_ref, v_ref, o_ref = refs
    else:
        q_ref, k_ref, v_ref, o_ref = refs
    q = q_ref[...]
    lk = k_ref.shape[0]
    dv = o_ref.shape[1]
    exp = jnp.exp2 if scale is None else jnp.exp
    m = l = acc = None
    nchunks = lk // tk

    def logits(c):
        s = _dot_nt(q, k_ref[c * tk:(c + 1) * tk, :])
        return s if scale is None else s * scale

    for c in range(nchunks):
        s = logits(c)
        mc = jnp.max(s, axis=1, keepdims=True)
        if c == 0:
            m_new = mc
            if has_sink:
                sink = sink_ref[pl.program_id(1)]
                m_new = jnp.maximum(m_new, sink)
        else:
            m_new = jnp.maximum(m, mc)
        p = exp(s - m_new)
        pv = _dot(p.astype(BF16), v_ref[c * tk:(c + 1) * tk, :])
        if not ones_col:
            ps = jnp.sum(p, axis=1, keepdims=True)
            if c == 0:
                l = ps + exp(sink - m_new) if has_sink else ps
            else:
                l = exp(m - m_new) * l + ps
        acc = pv if c == 0 else exp(m - m_new) * acc + pv
        m = m_new
    if ones_col:
        l = acc[:, dv:dv + 1]
        acc = acc[:, :dv]
    o_ref[...] = (acc / l).astype(o_ref.dtype)


def _dense_attention(q3, k3, v3, *, heads, groups, dk, dv, q_off, k_off, v_off, lk, tq, tk, scale, sink=None,
                     ones_col=False, name="dense_attn"):
    assert not (ones_col and sink is not None)
    bsz, lq, _ = q3.shape
    tq = _tile(lq, tq)
    dvw = 2 * dv if ones_col else dv
    in_specs = [pl.BlockSpec((None, tq, dk), lambda b, h, i: (b, i, q_off + h)),
                pl.BlockSpec((None, lk, dk), lambda b, h, i: (b, 0, k_off + h // groups)),
                pl.BlockSpec((None, lk, dvw), lambda b, h, i: (b, 0, v_off + h // groups))]
    args = [q3, k3, v3]
    if sink is not None:
        in_specs = [pl.BlockSpec(memory_space=pltpu.SMEM)] + in_specs
        args = [sink] + args
    return pl.pallas_call(
        functools.partial(_dense_body, tk=tk, scale=scale, has_sink=sink is not None, ones_col=ones_col),
        grid=(bsz, heads, lq // tq),
        in_specs=in_specs,
        out_specs=pl.BlockSpec((None, tq, dv), lambda b, h, i: (b, i, h)),
        out_shape=jax.ShapeDtypeStruct((bsz, lq, heads * dv), BF16),
        compiler_params=_params(3),
        name=name,
    )(*args)


def _na_body(q_ref, kp_ref, kc_ref, kn_ref, vp_ref, vc_ref, vn_ref, kx_ref, vx_ref, bias_ref, o_ref):
    heads = range(bias_ref.shape[0])
    cols = [slice(h * HEAD_DIM, (h + 1) * HEAD_DIM) for h in heads]
    s = [_dot_nt(q_ref[:, c], jnp.concatenate([kp_ref[:, c], kc_ref[:, c], kn_ref[:, c], kx_ref[:, c]], axis=0))
         + bias_ref[h] for h, c in zip(heads, cols)]
    p = [jnp.exp2(sh - jnp.max(sh, axis=1, keepdims=True)).astype(BF16) for sh in s]
    for h, c in zip(heads, cols):
        v_all = _with_ones_col(jnp.concatenate([vp_ref[:, c], vc_ref[:, c], vn_ref[:, c], vx_ref[:, c]], axis=0))
        acc = _dot(p[h], v_all)
        o_ref[:, c] = (acc[:, :HEAD_DIM] / acc[:, HEAD_DIM:HEAD_DIM + 1]).astype(o_ref.dtype)


def _na_valid(rows):
    r = NA_QROWS
    nblk = rows // r
    assert rows % r == 0 and nblk >= 3 and rows >= NA_ROWS and NA_ROWS >= 2 * r
    kwin = min(NA_ROWS, rows)
    valid = []
    for i in (0, 1, nblk - 1):
        qrow = (r * i + np.arange(r))[:, None, None, None]
        qcol = np.arange(GRID_W)[None, :, None, None]
        krow = (r * (i - 1) + np.arange(3 * r))[None, None, :, None]
        kcol = np.arange(GRID_W)[None, None, None, :]
        r0 = np.clip(qrow - kwin // 2, 0, rows - kwin)
        c0 = np.clip(qcol - NA_COLS // 2, 0, GRID_W - NA_COLS)
        ok = (krow >= 0) & (krow < rows) & (krow >= r0) & (krow < r0 + kwin) & (kcol >= c0) & (kcol < c0 + NA_COLS)
        valid.append(ok.reshape(r * GRID_W, 3 * r * GRID_W))
    return np.stack(valid)


def _na_bias_tables(rpb, rows, ctx):
    r, w = NA_QROWS, GRID_W
    heads, nro, nco = rpb.shape
    circ = jnp.concatenate([rpb[..., NA_COLS - 1:], jnp.zeros((heads, nro, 2 * w - nco), rpb.dtype),
                            rpb[..., :NA_COLS - 1]], axis=-1)
    toep = jnp.broadcast_to(circ[:, :, None, :], (heads, nro, w, 2 * w)).reshape(heads, nro, 2 * w * w)
    toep = toep[:, :, :w * (2 * w - 1)].reshape(heads, nro, w, 2 * w - 1)[..., :w]
    strips = []
    for qr in range(r):
        lo = NA_ROWS - 1 - r - qr
        strips.append(toep[:, lo:lo + 3 * r].transpose(0, 2, 1, 3).reshape(heads, w, 3 * r * w))
    tile = jnp.concatenate(strips, axis=1) * LOG2E
    local = jnp.where(jnp.asarray(_na_valid(rows))[None], tile[:, None], NEG_INF)
    return jnp.concatenate([local, jnp.zeros(local.shape[:3] + (ctx,), local.dtype)], axis=-1)


def _neighborhood_attention(plain, plain_c, rpb, bsz, seq, q_blk, k_blk, v_blk, heads_per_step):
    rows = seq // GRID_W
    tq = NA_QROWS * GRID_W
    nblk = rows // NA_QROWS
    ctx = plain_c.shape[0] // bsz
    hb = heads_per_step
    hw = hb * HEAD_DIM
    bias = _na_bias_tables(rpb, rows, ctx)

    def blk(off, d):
        def imap(b, h, i):
            return (b * nblk + jnp.clip(i + d, 0, nblk - 1), off + h)
        return pl.BlockSpec((tq, hw), imap)

    def bias_map(b, h, i):
        return (h, jnp.where(i == 0, 0, jnp.where(i == nblk - 1, 2, 1)), 0, 0)

    in_specs = [blk(q_blk, 0),
                blk(k_blk, -1), blk(k_blk, 0), blk(k_blk, 1),
                blk(v_blk, -1), blk(v_blk, 0), blk(v_blk, 1),
                pl.BlockSpec((ctx, hw), lambda b, h, i: (b, k_blk + h)),
                pl.BlockSpec((ctx, hw), lambda b, h, i: (b, v_blk + h)),
                pl.BlockSpec((hb, None, tq, 3 * tq + ctx), bias_map)]
    return pl.pallas_call(
        _na_body,
        grid=(bsz, C_HEADS // hb, nblk),
        in_specs=in_specs,
        out_specs=pl.BlockSpec((tq, hw), lambda b, h, i: (b * nblk + i, h)),
        out_shape=jax.ShapeDtypeStruct((bsz * seq, C_HEADS * HEAD_DIM), BF16),
        compiler_params=_params(3),
        name="na_attn",
    )(plain, plain, plain, plain, plain, plain, plain, plain_c, plain_c, bias)


def _rope_tables(seq):
    t = jnp.arange(seq, dtype=jnp.int32)
    row = (t // GRID_W).astype(F32)
    col = (t % GRID_W).astype(F32)

    def cs(rot_dim):
        n_freq = rot_dim // 4
        inv = jnp.power(ROPE_BASE, -jnp.arange(n_freq, dtype=F32) / n_freq)
        ang = jnp.concatenate([row[:, None] * inv, col[:, None] * inv], axis=-1)
        return jnp.cos(ang), jnp.sin(ang)

    ca, sa = cs(HEAD_DIM)
    cb, sb = cs(ROPE_DIM)
    zb = jnp.zeros_like(cb)
    return (jnp.concatenate([ca, ca], axis=1), jnp.concatenate([-sa, sa], axis=1),
            jnp.concatenate([cb, zb, cb, zb], axis=1), jnp.concatenate([-sb, zb, sb, zb], axis=1))


def _spread_rope_cols(w):
    h = ROPE_DIM // 2
    z = jnp.zeros(w.shape[:-1] + (LANE // 2 - h,), w.dtype)
    return jnp.concatenate([w[..., :h], z, w[..., h:], z], axis=-1)


def _stacked_weights(w_in, w_q_b, w_kv_b, w_br, w_out, w_ffn_gate, w_ffn_up, w_ffn_down):
    nl = w_in.shape[0]
    kvh = A_KV_HEADS * HEAD_DIM
    ch = C_HEADS * HEAD_DIM
    ah = A_HEADS * HEAD_DIM
    sizes = [kvh, kvh, KV_LORA, ROPE_DIM, ch, ch, ah, Q_LORA, ch, N_BRANCH * D_MODEL]
    offs = np.concatenate([[0], np.cumsum(sizes)])
    a_k, a_v, b_ckv, b_kr, c_k, c_v, a_q, b_qa, c_q, gate = [w_in[..., offs[i]:offs[i + 1]]
                                                             for i in range(len(sizes))]
    return {
        "rope": jnp.concatenate([a_k, a_q], axis=-1).astype(BF16),
        "plain": jnp.concatenate([a_v, c_k, c_v, c_q], axis=-1).astype(BF16),
        "ckv": jnp.concatenate([b_ckv, _spread_rope_cols(b_kr)], axis=-1).astype(BF16),
        "bqa": b_qa.astype(BF16),
        "gate": gate.astype(BF16),
        "qb": jnp.concatenate([w_q_b[..., :NOPE_DIM], _spread_rope_cols(w_q_b[..., NOPE_DIM:])], axis=-1)
        .reshape(nl, Q_LORA, B_HEADS * MLA_QK).astype(BF16),
        "uk": w_kv_b[..., :NOPE_DIM].reshape(nl, KV_LORA, B_HEADS * NOPE_DIM).astype(BF16),
        "uv": w_kv_b[..., NOPE_DIM:].reshape(nl, KV_LORA, B_HEADS * V_DIM).astype(BF16),
        "br": w_br.astype(BF16),
        "out": w_out.astype(BF16),
        "ffn_gate": w_ffn_gate.astype(BF16),
        "ffn_up": w_ffn_up.astype(BF16),
        "ffn_down": w_ffn_down.astype(BF16),
    }


def _layer(x2, ctx2, cv, tabs, layer, sw, w_ada, b_ada, g_mix, sink_a, g_q_a, g_kv_a, rpb_c, g_ffn, last):
    bsz, seq, ctx, d = BATCH, SEQ, CTX_LEN, D_MODEL
    cos_a, sin_a, cos_b, sin_b = tabs
    kvb = A_KV_HEADS * HEAD_DIM // LANE
    chb = C_HEADS * HEAD_DIM // LANE
    na_hb = math.gcd(math.gcd(kvb, chb), 4)

    mod = _adaln(cv, w_ada, b_ada, layer)
    sh_m, sc_m, gt_m, sh_f, sc_f, gt_f = [mod[:bsz, i * d:(i + 1) * d].reshape(bsz, 1, d) for i in range(6)]
    csh_m, csc_m, cgt_m, csh_f, csc_f, cgt_f = [mod[bsz:bsz + 1, i * d:(i + 1) * d].reshape(1, 1, d) for i in range(6)]
    w = {name: _LayerWeight(arr, layer) for name, arr in sw.items() if name != "br"}

    cm = bsz * ctx
    hc = _norm_mod(ctx2, g_mix, csh_m, csc_m, cm)
    ropecols_c = _mm_plain(hc, w["rope"] if not last else w["rope"].cols(kvb * LANE), bm=cm, name="mm_plain_c")
    plain_c = _mm_plain(hc, w["plain"] if not last else w["plain"].cols((kvb + 2 * chb) * LANE), bm=cm,
                        name="mm_plain_c")
    ckv_c, kr_c = _mm_ckv(hc, w["ckv"], g_kv_a, None, None, cm, rope=False, bm=cm)

    hx = _norm_mod(x2, g_mix, sh_m, sc_m, seq)
    qscale = jnp.full((1, A_HEADS * HEAD_DIM), HEAD_LOGIT_SCALE, F32)
    cs_rope = jnp.concatenate([jnp.ones((1, kvb * LANE), F32), qscale], axis=1)
    cs_plain = jnp.concatenate([jnp.ones((1, (kvb + 2 * chb) * LANE), F32),
                                jnp.full((1, chb * LANE), HEAD_LOGIT_SCALE, F32)], axis=1)
    qk_a = _mm_rope(hx, w["rope"], cs_rope, cos_a, sin_a, seq)
    plain = _mm_plain(hx, w["plain"], cs_plain)
    ckv, kr = _mm_ckv(hx, w["ckv"], g_kv_a, cos_b, sin_b, seq, rope=True)
    bqa = _mm_rms(hx, w["bqa"], g_q_a)
    gates = _mm_sigmoid(hx, w["gate"])
    qb = _mm_qb(bqa, w["qb"], cos_b, sin_b, seq, rope=True)

    lk = ctx + seq
    ckv_all = jnp.concatenate([ckv_c.reshape(bsz, ctx, -1), ckv.reshape(bsz, seq, -1)], axis=1).reshape(bsz * lk, -1)
    kr_all = jnp.concatenate([kr_c.reshape(bsz, ctx, -1), kr.reshape(bsz, seq, -1)], axis=1).reshape(bsz * lk, -1)
    k_exp = _mm_kexp(ckv_all, w["uk"], kr_all).reshape(bsz, lk, -1)
    v_exp = _mm_vexp(ckv_all, w["uv"]).reshape(bsz, lk, -1)

    o_a = _window_attention(sink_a, qk_a, plain, ropecols_c, plain_c, bsz, seq, kvb)
    o_b = _dense_attention(qb.reshape(bsz, seq, -1), k_exp, v_exp, heads=B_HEADS, groups=1, dk=MLA_QK, dv=V_DIM,
                           q_off=0, k_off=0, v_off=0, lk=lk, tq=1024, tk=_mla_chunk(lk), scale=None, ones_col=True,
                           name="mla_attn").reshape(bsz * seq, -1)
    o_c = _neighborhood_attention(plain, plain_c, rpb_c, bsz, seq, q_blk=(kvb + 2 * chb) // na_hb, k_blk=kvb // na_hb,
                                  v_blk=(kvb + chb) // na_hb, heads_per_step=na_hb)

    y = _mm_merge(o_a, o_b, o_c, sw["br"], layer, gates)
    x2 = _mm_resid(y, w["out"], x2, gt_m, seq)
    h2 = _norm_mod(x2, g_ffn, sh_f, sc_f, seq)
    act = _mm_ffn_up(h2, w["ffn_gate"], w["ffn_up"])
    x2 = _mm_resid(act, w["ffn_down"], x2, gt_f, seq, bm=512, bn=512, name="mm_ffn_down")

    if not last:
        gates_c = _mm_sigmoid(hc, w["gate"], bm=cm)
        bqa_c = _mm_rms(hc, w["bqa"], g_q_a, bm=cm)
        qb_c = _mm_qb(bqa_c, w["qb"], None, None, cm, rope=False, bm=cm)
        rc3 = ropecols_c.reshape(bsz, ctx, -1)
        pc3 = plain_c.reshape(bsz, ctx, -1)
        oc_a = _dense_attention(rc3, rc3, pc3, heads=A_HEADS, groups=A_HEADS // A_KV_HEADS, dk=HEAD_DIM, dv=HEAD_DIM,
                                q_off=kvb, k_off=0, v_off=0, lk=ctx, tq=ctx, tk=ctx, scale=HEAD_DIM ** -0.5,
                                sink=sink_a, name="ctx_attn_a").reshape(bsz * ctx, -1)
        oc_b = _dense_attention(qb_c.reshape(bsz, ctx, -1), k_exp, v_exp, heads=B_HEADS, groups=1, dk=MLA_QK,
                                dv=V_DIM, q_off=0, k_off=0, v_off=0, lk=ctx, tq=ctx, tk=ctx, scale=None, ones_col=True,
                                name="ctx_attn_b").reshape(bsz * ctx, -1)
        oc_c = _dense_attention(pc3, pc3, pc3, heads=C_HEADS, groups=1, dk=HEAD_DIM, dv=HEAD_DIM,
                                q_off=kvb + 2 * chb, k_off=kvb, v_off=kvb + chb, lk=ctx, tq=ctx, tk=ctx,
                                scale=HEAD_DIM ** -0.5, name="ctx_attn_c").reshape(bsz * ctx, -1)
        yc = _mm_merge(oc_a, oc_b, oc_c, sw["br"], layer, gates_c, bm=cm, bn=512)
        ctx2 = _mm_resid(yc, w["out"], ctx2, cgt_m, cm, bm=cm, name="mm_resid_c")
        hc2 = _norm_mod(ctx2, g_ffn, csh_f, csc_f, cm)
        act_c = _mm_ffn_up(hc2, w["ffn_gate"], w["ffn_up"], bm=cm)
        ctx2 = _mm_resid(act_c, w["ffn_down"], ctx2, cgt_f, cm, bm=cm, bn=256, name="mm_ffn_down_c")
    return x2, ctx2


def _mla_chunk(lk):
    for parts in range(40, 0, -1):
        if lk % parts == 0 and (lk // parts) % (2 * LANE) == 0:
            return lk // parts
    return lk


def _window_attention(sink, qk_a, plain, ropecols_c, plain_c, bsz, seq, kvb):
    nblk = seq // BLOCK
    groups = A_HEADS // A_KV_HEADS
    gw = groups * HEAD_DIM
    ctx = plain_c.shape[0] // bsz

    def blk(d):
        def imap(b, h, n):
            return (b * nblk + jnp.clip(n + d, 0, nblk - 1), h)
        return pl.BlockSpec((BLOCK, HEAD_DIM), imap)

    cspec = pl.BlockSpec((ctx, HEAD_DIM), lambda b, h, n: (b, h))
    mspec = pl.BlockSpec((None, BLOCK, 3 * BLOCK + ctx),
                         lambda b, h, n: (jnp.where(n == 0, 0, jnp.where(n == nblk - 1, 2, 1)), 0, 0))
    in_specs = [pl.BlockSpec(memory_space=pltpu.SMEM),
                pl.BlockSpec((BLOCK, gw), lambda b, h, n: (b * nblk + n, kvb // groups + h)),
                blk(-1), blk(0), blk(1), blk(-1), blk(0), blk(1), cspec, cspec, mspec]
    return pl.pallas_call(
        functools.partial(_win_body, groups=groups),
        grid=(bsz, A_KV_HEADS, nblk),
        in_specs=in_specs,
        out_specs=pl.BlockSpec((BLOCK, gw), lambda b, h, n: (b * nblk + n, h)),
        out_shape=jax.ShapeDtypeStruct((bsz * seq, A_HEADS * HEAD_DIM), BF16),
        compiler_params=_params(3),
        name="win_attn",
    )(sink, qk_a, qk_a, qk_a, qk_a, plain, plain, plain, ropecols_c, plain_c, jnp.asarray(_win_masks(ctx)))


def kernel(x, c, ctx, c_ctx, w_ada, b_ada, g_mix, w_in, sink_a, g_q_a, w_q_b, g_kv_a, w_kv_b, rpb_c, w_br, w_out,
           g_ffn, w_ffn_gate, w_ffn_up, w_ffn_down, g_final):
    bsz, seq, d = x.shape
    tabs = _rope_tables(seq)
    pad = (-(bsz + 1)) % 8
    cv = jnp.concatenate([c, c_ctx[None, :], jnp.zeros((pad, d), F32)], axis=0)
    x2 = x.reshape(bsz * seq, d)
    ctx2 = ctx.reshape(bsz * ctx.shape[1], d)
    sw = _stacked_weights(w_in, w_q_b, w_kv_b, w_br, w_out, w_ffn_gate, w_ffn_up, w_ffn_down)
    for l in range(DEPTH):
        x2, ctx2 = _layer(x2, ctx2, cv, tabs, l, sw, w_ada, b_ada, g_mix[l], sink_a[l], g_q_a[l], g_kv_a[l], rpb_c[l],
                          g_ffn[l], last=(l == DEPTH - 1))
    zero = jnp.zeros((1, 1, d), F32)
    out = _norm_mod(x2, g_final, zero, zero, seq, out_dtype=F32)
    return out.reshape(bsz, seq, d)
```

```python
import functools
import math

import numpy as np
import jax
import jax.numpy as jnp
from jax import lax
from jax.experimental import pallas as pl
from jax.experimental.pallas import tpu as pltpu

D_MODEL = 4096
BATCH = 2
SEQ = 8192
DEPTH = 2
GRID_W = 64
CTX_LEN = 256
HEAD_DIM = 128
A_HEADS = 16
A_KV_HEADS = 4
WINDOW = 128
BLOCK = 128
B_HEADS = 16
Q_LORA = 1024
KV_LORA = 512
NOPE_DIM = 128
ROPE_DIM = 64
V_DIM = 128
C_HEADS = 16
NA_ROWS = 8
NA_COLS = 16
N_BRANCH = 3
ROPE_BASE = 10000.0
EPS = 1e-6
NEG_INF = -1e30

LANE = 128
MLA_QK = 2 * LANE
LOG2E = math.log2(math.e)
MLA_LOGIT_SCALE = (NOPE_DIM + ROPE_DIM) ** -0.5 * LOG2E
HEAD_LOGIT_SCALE = HEAD_DIM ** -0.5 * LOG2E
NA_QROWS = 4
VMEM_LIMIT = 56 * 1024 * 1024

BF16 = jnp.bfloat16
F32 = jnp.float32


def _params(n_axes):
    return pltpu.CompilerParams(dimension_semantics=("parallel",) * n_axes,
                                vmem_limit_bytes=VMEM_LIMIT)


def _dot(a, b):
    return jnp.dot(a, b, preferred_element_type=F32)


def _dot_nt(a, b):
    return lax.dot_general(a, b, (((1,), (1,)), ((), ())), preferred_element_type=F32)


def _tile(n, pref):
    if n <= pref:
        return n
    t = pref - pref % LANE
    while t >= LANE:
        if n % t == 0:
            return t
        t -= LANE
    return n


def _adaln_body(c_ref, w_ref, b_ref, o_ref):
    cv = c_ref[...]
    o_ref[...] = _dot(cv * jax.nn.sigmoid(cv), w_ref[...]) + b_ref[...]


def _adaln(cv, w_ada, b_ada, layer):
    rows, d = cv.shape
    n = w_ada.shape[2]
    bn = _tile(n, 512)
    return pl.pallas_call(
        _adaln_body,
        grid=(n // bn,),
        in_specs=[pl.BlockSpec((rows, d), lambda j: (0, 0)),
                  pl.BlockSpec((None, d, bn), lambda j: (layer, 0, j)),
                  pl.BlockSpec((None, 1, bn), lambda j: (layer, 0, j))],
        out_specs=pl.BlockSpec((rows, bn), lambda j: (0, j)),
        out_shape=jax.ShapeDtypeStruct((rows, n), F32),
        compiler_params=_params(1),
        name="adaln",
    )(cv, w_ada, b_ada.reshape(b_ada.shape[0], 1, n))


def _norm_mod_body(x_ref, g_ref, sh_ref, sc_ref, o_ref):
    x = x_ref[...]
    y = x * lax.rsqrt(jnp.mean(x * x, axis=-1, keepdims=True) + EPS) * g_ref[...]
    o_ref[...] = (y * (1.0 + sc_ref[...]) + sh_ref[...]).astype(o_ref.dtype)


def _norm_mod(x2, g, shift, scale, rows_per_batch, out_dtype=BF16):
    m, d = x2.shape
    ts = _tile(rows_per_batch, 512)
    tpb = rows_per_batch // ts
    per_batch = shift.shape[0] > 1
    mod_spec = pl.BlockSpec((None, 1, d), (lambda i: (i // tpb, 0, 0)) if per_batch else (lambda i: (0, 0, 0)))
    return pl.pallas_call(
        _norm_mod_body,
        grid=(m // ts,),
        in_specs=[pl.BlockSpec((ts, d), lambda i: (i, 0)),
                  pl.BlockSpec((1, d), lambda i: (0, 0)),
                  mod_spec, mod_spec],
        out_specs=pl.BlockSpec((ts, d), lambda i: (i, 0)),
        out_shape=jax.ShapeDtypeStruct((m, d), out_dtype),
        compiler_params=_params(1),
        name="norm_mod",
    )(x2, g.reshape(1, d), shift, scale)


def _a_spec(bm, k):
    return pl.BlockSpec((bm, k), lambda i, j: (i, 0))


class _LayerWeight:
    def __init__(self, arr, layer, ncols=None):
        self.arr, self.layer = arr, layer
        self.shape = (arr.shape[1], arr.shape[2] if ncols is None else ncols)

    def cols(self, ncols):
        return _LayerWeight(self.arr, self.layer, ncols)


def _w_spec(w, bn):
    return pl.BlockSpec((None, w.shape[0], bn), lambda i, j: (w.layer, 0, j))


def _mn_spec(bm, bn):
    return pl.BlockSpec((bm, bn), lambda i, j: (i, j))


def _pos_spec(bm, width, tiles_per_batch):
    return pl.BlockSpec((bm, width), lambda i, j: (i % tiles_per_batch, 0))


def _mm_call(body, grid, in_specs, out_specs, out_shape, args, name):
    args = [a.arr if isinstance(a, _LayerWeight) else a for a in args]
    return pl.pallas_call(body, grid=grid, in_specs=in_specs, out_specs=out_specs, out_shape=out_shape,
                          compiler_params=_params(2), name=name)(*args)


def _plain_body(a_ref, w_ref, o_ref):
    o_ref[...] = _dot(a_ref[...], w_ref[...]).astype(o_ref.dtype)


def _scaled_body(a_ref, w_ref, cs_ref, o_ref):
    o_ref[...] = (_dot(a_ref[...], w_ref[...]) * cs_ref[...]).astype(o_ref.dtype)


def _n_spec(bn):
    return pl.BlockSpec((1, bn), lambda i, j: (0, j))


def _mm_plain(a, w, colscale=None, bm=2048, bn=512, name="mm_plain"):
    m, k = a.shape
    n = w.shape[1]
    bm, bn = _tile(m, bm), _tile(n, bn)
    if colscale is None:
        return _mm_call(_plain_body, (m // bm, n // bn), [_a_spec(bm, k), _w_spec(w, bn)], _mn_spec(bm, bn),
                        jax.ShapeDtypeStruct((m, n), BF16), (a, w), name)
    return _mm_call(_scaled_body, (m // bm, n // bn), [_a_spec(bm, k), _w_spec(w, bn), _n_spec(bn)],
                    _mn_spec(bm, bn), jax.ShapeDtypeStruct((m, n), BF16), (a, w, colscale), name)


def _rope_chunks(acc, cos, sin):
    outs = []
    for c in range(acc.shape[1] // LANE):
        t = acc[:, c * LANE:(c + 1) * LANE]
        outs.append(t * cos + pltpu.roll(t, LANE // 2, 1) * sin)
    return outs[0] if len(outs) == 1 else jnp.concatenate(outs, axis=1)


def _rope_body(a_ref, w_ref, cs_ref, cos_ref, sin_ref, o_ref):
    acc = _dot(a_ref[...], w_ref[...]) * cs_ref[...]
    o_ref[...] = _rope_chunks(acc, cos_ref[...], sin_ref[...]).astype(o_ref.dtype)


def _mm_rope(a, w, colscale, cos2, sin2, rows_per_batch, bm=2048, bn=512):
    m, k = a.shape
    n = w.shape[1]
    bm, bn = _tile(rows_per_batch, bm), _tile(n, bn)
    tpb = rows_per_batch // bm
    return _mm_call(_rope_body, (m // bm, n // bn),
                    [_a_spec(bm, k), _w_spec(w, bn), _n_spec(bn), _pos_spec(bm, LANE, tpb), _pos_spec(bm, LANE, tpb)],
                    _mn_spec(bm, bn), jax.ShapeDtypeStruct((m, n), BF16), (a, w, colscale, cos2, sin2), "mm_rope")


def _ckv_body(a_ref, w_ref, g_ref, *rest, rope):
    if rope:
        cos_ref, sin_ref, ckv_ref, kr_ref = rest
    else:
        ckv_ref, kr_ref = rest
    acc = _dot(a_ref[...], w_ref[...])
    nl = ckv_ref.shape[1]
    t = acc[:, :nl]
    ckv_ref[...] = (t * lax.rsqrt(jnp.mean(t * t, axis=-1, keepdims=True) + EPS) * g_ref[...]).astype(ckv_ref.dtype)
    r = acc[:, nl:]
    if rope:
        r = _rope_chunks(r, cos_ref[...], sin_ref[...])
    kr_ref[...] = r.astype(kr_ref.dtype)


def _mm_ckv(a, w, g, cos2, sin2, rows_per_batch, rope, bm=1024):
    m, k = a.shape
    n = w.shape[1]
    nl = n - LANE
    bm = _tile(rows_per_batch, bm)
    tpb = rows_per_batch // bm
    in_specs = [_a_spec(bm, k), _w_spec(w, n), pl.BlockSpec((1, nl), lambda i, j: (0, 0))]
    args = [a, w, g.reshape(1, nl)]
    if rope:
        in_specs += [_pos_spec(bm, LANE, tpb), _pos_spec(bm, LANE, tpb)]
        args += [cos2, sin2]
    return _mm_call(functools.partial(_ckv_body, rope=rope), (m // bm, 1), in_specs,
                    [pl.BlockSpec((bm, nl), lambda i, j: (i, 0)), pl.BlockSpec((bm, LANE), lambda i, j: (i, 0))],
                    [jax.ShapeDtypeStruct((m, nl), BF16), jax.ShapeDtypeStruct((m, LANE), BF16)], args, "mm_ckv")


def _rms_body(a_ref, w_ref, g_ref, o_ref):
    t = _dot(a_ref[...], w_ref[...])
    o_ref[...] = (t * lax.rsqrt(jnp.mean(t * t, axis=-1, keepdims=True) + EPS) * g_ref[...]).astype(o_ref.dtype)


def _mm_rms(a, w, g, bm=1024):
    m, k = a.shape
    n = w.shape[1]
    bm = _tile(m, bm)
    return _mm_call(_rms_body, (m // bm, 1), [_a_spec(bm, k), _w_spec(w, n), pl.BlockSpec((1, n), lambda i, j: (0, 0))],
                    _mn_spec(bm, n), jax.ShapeDtypeStruct((m, n), BF16), (a, w, g.reshape(1, n)), "mm_rms")


def _sigmoid_body(a_ref, w_ref, o_ref):
    o_ref[...] = jax.nn.sigmoid(_dot(a_ref[...], w_ref[...]))


def _mm_sigmoid(a, w, bm=1024, bn=1024):
    m, k = a.shape
    n = w.shape[1]
    bm, bn = _tile(m, bm), _tile(n, bn)
    return _mm_call(_sigmoid_body, (m // bm, n // bn), [_a_spec(bm, k), _w_spec(w, bn)], _mn_spec(bm, bn),
                    jax.ShapeDtypeStruct((m, n), F32), (a, w), "mm_gates")


def _qb_body(a_ref, w_ref, *rest, rope):
    if rope:
        cos_ref, sin_ref, o_ref = rest
    else:
        (o_ref,) = rest
    acc = _dot(a_ref[...], w_ref[...]) * MLA_LOGIT_SCALE
    if rope:
        parts = []
        for h in range(acc.shape[1] // MLA_QK):
            parts.append(acc[:, h * MLA_QK:h * MLA_QK + NOPE_DIM])
            parts.append(_rope_chunks(acc[:, h * MLA_QK + NOPE_DIM:(h + 1) * MLA_QK], cos_ref[...], sin_ref[...]))
        acc = jnp.concatenate(parts, axis=1)
    o_ref[...] = acc.astype(o_ref.dtype)


def _mm_qb(a, w, cos2, sin2, rows_per_batch, rope, bm=1024):
    m, k = a.shape
    n = w.shape[1]
    bm = _tile(rows_per_batch, bm)
    bn = 4 * MLA_QK if n % (4 * MLA_QK) == 0 else MLA_QK
    tpb = rows_per_batch // bm
    in_specs = [_a_spec(bm, k), _w_spec(w, bn)]
    args = [a, w]
    if rope:
        in_specs += [_pos_spec(bm, LANE, tpb), _pos_spec(bm, LANE, tpb)]
        args += [cos2, sin2]
    return _mm_call(functools.partial(_qb_body, rope=rope), (m // bm, n // bn), in_specs, _mn_spec(bm, bn),
                    jax.ShapeDtypeStruct((m, n), BF16), args, "mm_qb")


def _kexp_body(a_ref, w_ref, kr_ref, o_ref):
    acc = _dot(a_ref[...], w_ref[...]).astype(o_ref.dtype)
    kr = kr_ref[...]
    parts = []
    for h in range(acc.shape[1] // NOPE_DIM):
        parts += [acc[:, h * NOPE_DIM:(h + 1) * NOPE_DIM], kr]
    o_ref[...] = jnp.concatenate(parts, axis=1)


def _mm_kexp(ckv, w_uk, kr, bm=1536, heads_per_tile=8):
    m, k = ckv.shape
    n = w_uk.shape[1]
    heads = n // NOPE_DIM
    bm = _tile(m, bm)
    hpt = heads_per_tile if heads % heads_per_tile == 0 else 1
    return _mm_call(_kexp_body, (m // bm, heads // hpt),
                    [_a_spec(bm, k), _w_spec(w_uk, hpt * NOPE_DIM), pl.BlockSpec((bm, LANE), lambda i, j: (i, 0))],
                    _mn_spec(bm, hpt * MLA_QK), jax.ShapeDtypeStruct((m, heads * MLA_QK), BF16),
                    (ckv, w_uk, kr), "mm_kexp")


def _vexp_body(a_ref, w_ref, o_ref):
    acc = _dot(a_ref[...], w_ref[...]).astype(o_ref.dtype)
    lane = lax.broadcasted_iota(jnp.int32, (acc.shape[0], V_DIM), 1)
    ones = jnp.where(lane == 0, 1.0, 0.0).astype(o_ref.dtype)
    parts = []
    for h in range(acc.shape[1] // V_DIM):
        parts += [acc[:, h * V_DIM:(h + 1) * V_DIM], ones]
    o_ref[...] = jnp.concatenate(parts, axis=1)


def _mm_vexp(ckv, w_uv, bm=1536, heads_per_tile=8):
    m, k = ckv.shape
    heads = w_uv.shape[1] // V_DIM
    bm = _tile(m, bm)
    hpt = heads_per_tile if heads % heads_per_tile == 0 else 1
    return _mm_call(_vexp_body, (m // bm, heads // hpt), [_a_spec(bm, k), _w_spec(w_uv, hpt * V_DIM)],
                    _mn_spec(bm, 2 * hpt * V_DIM), jax.ShapeDtypeStruct((m, 2 * heads * V_DIM), BF16),
                    (ckv, w_uv), "mm_vexp")


def _merge_body(oa_ref, ob_ref, oc_ref, wa_ref, wb_ref, wc_ref, ga_ref, gb_ref, gc_ref, o_ref):
    y = ga_ref[...] * _dot(oa_ref[...], wa_ref[...])
    y = y + gb_ref[...] * _dot(ob_ref[...], wb_ref[...])
    y = y + gc_ref[...] * _dot(oc_ref[...], wc_ref[...])
    o_ref[...] = y.astype(o_ref.dtype)


def _mm_merge(o_a, o_b, o_c, w_br, layer, gates, bm=1024, bn=256):
    m, k = o_a.shape
    n = w_br.shape[3]
    bm, bn = _tile(m, bm), _tile(n, bn)
    nb = n // bn
    wspec = [pl.BlockSpec((None, None, k, bn), (lambda i, j, r=r: (layer, r, 0, j))) for r in range(N_BRANCH)]
    gspec = [pl.BlockSpec((bm, bn), (lambda i, j, r=r: (i, r * nb + j))) for r in range(N_BRANCH)]
    return _mm_call(_merge_body, (m // bm, nb), [_a_spec(bm, k)] * 3 + wspec + gspec, _mn_spec(bm, bn),
                    jax.ShapeDtypeStruct((m, n), BF16), (o_a, o_b, o_c, w_br, w_br, w_br, gates, gates, gates),
                    "mm_merge")


def _resid_body(a_ref, w_ref, x_ref, gt_ref, o_ref):
    o_ref[...] = x_ref[...] + gt_ref[...] * _dot(a_ref[...], w_ref[...])


def _mm_resid(a, w, x2, gate, rows_per_batch, bm=1024, bn=512, name="mm_resid"):
    m, k = a.shape
    n = w.shape[1]
    bm, bn = _tile(rows_per_batch, bm), _tile(n, bn)
    tpb = rows_per_batch // bm
    per_batch = gate.shape[0] > 1
    gspec = pl.BlockSpec((None, 1, bn), (lambda i, j: (i // tpb, 0, j)) if per_batch else (lambda i, j: (0, 0, j)))
    return _mm_call(_resid_body, (m // bm, n // bn), [_a_spec(bm, k), _w_spec(w, bn), _mn_spec(bm, bn), gspec],
                    _mn_spec(bm, bn), jax.ShapeDtypeStruct((m, n), F32), (a, w, x2, gate), name)


def _ffn_up_body(a_ref, wg_ref, wu_ref, o_ref):
    a = a_ref[...]
    g = _dot(a, wg_ref[...])
    o_ref[...] = (g * jax.nn.sigmoid(g) * _dot(a, wu_ref[...])).astype(o_ref.dtype)


def _mm_ffn_up(a, wg, wu, bm=2048, bn=256):
    m, k = a.shape
    n = wg.shape[1]
    bm, bn = _tile(m, bm), _tile(n, bn)
    return _mm_call(_ffn_up_body, (m // bm, n // bn), [_a_spec(bm, k), _w_spec(wg, bn), _w_spec(wu, bn)],
                    _mn_spec(bm, bn), jax.ShapeDtypeStruct((m, n), BF16), (a, wg, wu), "mm_ffn_up")


def _with_ones_col(v):
    lane = lax.broadcasted_iota(jnp.int32, v.shape, 1)
    return jnp.concatenate([v, jnp.where(lane == 0, 1.0, 0.0).astype(v.dtype)], axis=1)


def _win_body(sink_ref, *refs, groups, kv_per_step):
    q_refs = refs[:kv_per_step]
    kp_ref, kc_ref, kn_ref, vp_ref, vc_ref, vn_ref, kx_ref, vx_ref, mask_ref, o_ref = refs[kv_per_step:]
    kv0 = pl.program_id(1) * kv_per_step
    mask = mask_ref[...]
    chains = [(j, g) for j in range(kv_per_step) for g in range(groups)]
    k_all, v_all = [], []
    for j in range(kv_per_step):
        c = slice(j * HEAD_DIM, (j + 1) * HEAD_DIM)
        k_all.append(jnp.concatenate([kp_ref[:, c], kc_ref[:, c], kn_ref[:, c], kx_ref[:, c]], axis=0))
        v_all.append(_with_ones_col(jnp.concatenate([vp_ref[:, c], vc_ref[:, c], vn_ref[:, c], vx_ref[:, c]], axis=0)))
    s = [_dot_nt(q_refs[j][:, g * HEAD_DIM:(g + 1) * HEAD_DIM], k_all[j]) + mask for j, g in chains]
    sink = [sink_ref[(kv0 + j) * groups + g] * LOG2E for j, g in chains]
    m = [jnp.maximum(jnp.max(si, axis=1, keepdims=True), sk) for si, sk in zip(s, sink)]
    p = [jnp.exp2(si - mi).astype(BF16) for si, mi in zip(s, m)]
    for i, (j, g) in enumerate(chains):
        acc = _dot(p[i], v_all[j])
        l = acc[:, HEAD_DIM:HEAD_DIM + 1] + jnp.exp2(sink[i] - m[i])
        c0 = (j * groups + g) * HEAD_DIM
        o_ref[:, c0:c0 + HEAD_DIM] = (acc[:, :HEAD_DIM] / l).astype(o_ref.dtype)


def _win_masks(ctx):
    qi = np.arange(BLOCK)[:, None]
    kj = np.arange(3 * BLOCK)[None, :]
    band = np.abs(kj - qi - BLOCK) <= WINDOW
    out = []
    for lo, hi in ((BLOCK, 3 * BLOCK), (0, 3 * BLOCK), (0, 2 * BLOCK)):
        ok = band & (kj >= lo) & (kj < hi)
        out.append(np.concatenate([np.where(ok, 0.0, NEG_INF), np.zeros((BLOCK, ctx))], axis=1))
    return np.stack(out).astype(np.float32)


def _dense_body(*refs, tk, scale, has_sink, ones_col):
    if has_sink:
        sink_ref, q_ref, k_ref, v_ref, o_ref = refs
    else:
        q_ref, k_ref, v_ref, o_ref = refs
    q = q_ref[...]
    lk = k_ref.shape[0]
    dv = o_ref.shape[1]
    exp = jnp.exp2 if scale is None else jnp.exp
    m = l = acc = None
    nchunks = lk // tk

    def logits(c):
        s = _dot_nt(q, k_ref[c * tk:(c + 1) * tk, :])
        return s if scale is None else s * scale

    for c in range(nchunks):
        s = logits(c)
        mc = jnp.max(s, axis=1, keepdims=True)
        if c == 0:
            m_new = mc
            if has_sink:
                sink = sink_ref[pl.program_id(1)]
                m_new = jnp.maximum(m_new, sink)
        else:
            m_new = jnp.maximum(m, mc)
        p = exp(s - m_new)
        pv = _dot(p.astype(BF16), v_ref[c * tk:(c + 1) * tk, :])
        if not ones_col:
            ps = jnp.sum(p, axis=1, keepdims=True)
            if c == 0:
                l = ps + exp(sink - m_new) if has_sink else ps
            else:
                l = exp(m - m_new) * l + ps
        acc = pv if c == 0 else exp(m - m_new) * acc + pv
        m = m_new
    if ones_col:
        l = acc[:, dv:dv + 1]
        acc = acc[:, :dv]
    o_ref[...] = (acc / l).astype(o_ref.dtype)


def _dense_attention(q3, k3, v3, *, heads, groups, dk, dv, q_off, k_off, v_off, lk, tq, tk, scale, sink=None,
                     ones_col=False, name="dense_attn"):
    assert not (ones_col and sink is not None)
    bsz, lq, _ = q3.shape
    tq = _tile(lq, tq)
    dvw = 2 * dv if ones_col else dv
    in_specs = [pl.BlockSpec((None, tq, dk), lambda b, h, i: (b, i, q_off + h)),
                pl.BlockSpec((None, lk, dk), lambda b, h, i: (b, 0, k_off + h // groups)),
                pl.BlockSpec((None, lk, dvw), lambda b, h, i: (b, 0, v_off + h // groups))]
    args = [q3, k3, v3]
    if sink is not None:
        in_specs = [pl.BlockSpec(memory_space=pltpu.SMEM)] + in_specs
        args = [sink] + args
    return pl.pallas_call(
        functools.partial(_dense_body, tk=tk, scale=scale, has_sink=sink is not None, ones_col=ones_col),
        grid=(bsz, heads, lq // tq),
        in_specs=in_specs,
        out_specs=pl.BlockSpec((None, tq, dv), lambda b, h, i: (b, i, h)),
        out_shape=jax.ShapeDtypeStruct((bsz, lq, heads * dv), BF16),
        compiler_params=_params(3),
        name=name,
    )(*args)


def _na_body(q_ref, kp_ref, kc_ref, kn_ref, vp_ref, vc_ref, vn_ref, kx_ref, vx_ref, bias_ref, o_ref):
    heads = range(bias_ref.shape[0])
    cols = [slice(h * HEAD_DIM, (h + 1) * HEAD_DIM) for h in heads]
    s = [_dot_nt(q_ref[:, c], jnp.concatenate([kp_ref[:, c], kc_ref[:, c], kn_ref[:, c], kx_ref[:, c]], axis=0))
         + bias_ref[h] for h, c in zip(heads, cols)]
    p = [jnp.exp2(sh - jnp.max(sh, axis=1, keepdims=True)).astype(BF16) for sh in s]
    for h, c in zip(heads, cols):
        v_all = _with_ones_col(jnp.concatenate([vp_ref[:, c], vc_ref[:, c], vn_ref[:, c], vx_ref[:, c]], axis=0))
        acc = _dot(p[h], v_all)
        o_ref[:, c] = (acc[:, :HEAD_DIM] / acc[:, HEAD_DIM:HEAD_DIM + 1]).astype(o_ref.dtype)


def _na_valid(rows):
    r = NA_QROWS
    nblk = rows // r
    assert rows % r == 0 and nblk >= 3 and rows >= NA_ROWS and NA_ROWS >= 2 * r
    kwin = min(NA_ROWS, rows)
    valid = []
    for i in (0, 1, nblk - 1):
        qrow = (r * i + np.arange(r))[:, None, None, None]
        qcol = np.arange(GRID_W)[None, :, None, None]
        krow = (r * (i - 1) + np.arange(3 * r))[None, None, :, None]
        kcol = np.arange(GRID_W)[None, None, None, :]
        r0 = np.clip(qrow - kwin // 2, 0, rows - kwin)
        c0 = np.clip(qcol - NA_COLS // 2, 0, GRID_W - NA_COLS)
        ok = (krow >= 0) & (krow < rows) & (krow >= r0) & (krow < r0 + kwin) & (kcol >= c0) & (kcol < c0 + NA_COLS)
        valid.append(ok.reshape(r * GRID_W, 3 * r * GRID_W))
    return np.stack(valid)


def _na_bias_tables(rpb, rows, ctx):
    r, w = NA_QROWS, GRID_W
    heads, nro, nco = rpb.shape
    circ = jnp.concatenate([rpb[..., NA_COLS - 1:], jnp.zeros((heads, nro, 2 * w - nco), rpb.dtype),
                            rpb[..., :NA_COLS - 1]], axis=-1)
    toep = jnp.broadcast_to(circ[:, :, None, :], (heads, nro, w, 2 * w)).reshape(heads, nro, 2 * w * w)
    toep = toep[:, :, :w * (2 * w - 1)].reshape(heads, nro, w, 2 * w - 1)[..., :w]
    strips = []
    for qr in range(r):
        lo = NA_ROWS - 1 - r - qr
        strips.append(toep[:, lo:lo + 3 * r].transpose(0, 2, 1, 3).reshape(heads, w, 3 * r * w))
    tile = jnp.concatenate(strips, axis=1) * LOG2E
    local = jnp.where(jnp.asarray(_na_valid(rows))[None], tile[:, None], NEG_INF)
    return jnp.concatenate([local, jnp.zeros(local.shape[:3] + (ctx,), local.dtype)], axis=-1)


def _neighborhood_attention(plain, plain_c, rpb, bsz, seq, q_blk, k_blk, v_blk, heads_per_step):
    rows = seq // GRID_W
    tq = NA_QROWS * GRID_W
    nblk = rows // NA_QROWS
    ctx = plain_c.shape[0] // bsz
    hb = heads_per_step
    hw = hb * HEAD_DIM
    bias = _na_bias_tables(rpb, rows, ctx)

    def blk(off, d):
        def imap(b, h, i):
            return (b * nblk + jnp.clip(i + d, 0, nblk - 1), off + h)
        return pl.BlockSpec((tq, hw), imap)

    def bias_map(b, h, i):
        return (h, jnp.where(i == 0, 0, jnp.where(i == nblk - 1, 2, 1)), 0, 0)

    in_specs = [blk(q_blk, 0),
                blk(k_blk, -1), blk(k_blk, 0), blk(k_blk, 1),
                blk(v_blk, -1), blk(v_blk, 0), blk(v_blk, 1),
                pl.BlockSpec((ctx, hw), lambda b, h, i: (b, k_blk + h)),
                pl.BlockSpec((ctx, hw), lambda b, h, i: (b, v_blk + h)),
                pl.BlockSpec((hb, None, tq, 3 * tq + ctx), bias_map)]
    return pl.pallas_call(
        _na_body,
        grid=(bsz, C_HEADS // hb, nblk),
        in_specs=in_specs,
        out_specs=pl.BlockSpec((tq, hw), lambda b, h, i: (b * nblk + i, h)),
        out_shape=jax.ShapeDtypeStruct((bsz * seq, C_HEADS * HEAD_DIM), BF16),
        compiler_params=_params(3),
        name="na_attn",
    )(plain, plain, plain, plain, plain, plain, plain, plain_c, plain_c, bias)


def _rope_tables(seq):
    t = jnp.arange(seq, dtype=jnp.int32)
    row = (t // GRID_W).astype(F32)
    col = (t % GRID_W).astype(F32)

    def cs(rot_dim):
        n_freq = rot_dim // 4
        inv = jnp.power(ROPE_BASE, -jnp.arange(n_freq, dtype=F32) / n_freq)
        ang = jnp.concatenate([row[:, None] * inv, col[:, None] * inv], axis=-1)
        return jnp.cos(ang), jnp.sin(ang)

    ca, sa = cs(HEAD_DIM)
    cb, sb = cs(ROPE_DIM)
    zb = jnp.zeros_like(cb)
    return (jnp.concatenate([ca, ca], axis=1), jnp.concatenate([-sa, sa], axis=1),
            jnp.concatenate([cb, zb, cb, zb], axis=1), jnp.concatenate([-sb, zb, sb, zb], axis=1))


def _spread_rope_cols(w):
    h = ROPE_DIM // 2
    z = jnp.zeros(w.shape[:-1] + (LANE // 2 - h,), w.dtype)
    return jnp.concatenate([w[..., :h], z, w[..., h:], z], axis=-1)


def _stacked_weights(w_in, w_q_b, w_kv_b, w_br, w_out, w_ffn_gate, w_ffn_up, w_ffn_down):
    nl = w_in.shape[0]
    kvh = A_KV_HEADS * HEAD_DIM
    ch = C_HEADS * HEAD_DIM
    ah = A_HEADS * HEAD_DIM
    sizes = [kvh, kvh, KV_LORA, ROPE_DIM, ch, ch, ah, Q_LORA, ch, N_BRANCH * D_MODEL]
    offs = np.concatenate([[0], np.cumsum(sizes)])
    a_k, a_v, b_ckv, b_kr, c_k, c_v, a_q, b_qa, c_q, gate = [w_in[..., offs[i]:offs[i + 1]]
                                                             for i in range(len(sizes))]
    return {
        "rope": jnp.concatenate([a_k, a_q], axis=-1).astype(BF16),
        "plain": jnp.concatenate([c_k, c_v, c_q, a_v], axis=-1).astype(BF16),
        "ckv": jnp.concatenate([b_ckv, _spread_rope_cols(b_kr)], axis=-1).astype(BF16),
        "bqa": b_qa.astype(BF16),
        "gate": gate.astype(BF16),
        "qb": jnp.concatenate([w_q_b[..., :NOPE_DIM], _spread_rope_cols(w_q_b[..., NOPE_DIM:])], axis=-1)
        .reshape(nl, Q_LORA, B_HEADS * MLA_QK).astype(BF16),
        "uk": w_kv_b[..., :NOPE_DIM].reshape(nl, KV_LORA, B_HEADS * NOPE_DIM).astype(BF16),
        "uv": w_kv_b[..., NOPE_DIM:].reshape(nl, KV_LORA, B_HEADS * V_DIM).astype(BF16),
        "br": w_br.astype(BF16),
        "out": w_out.astype(BF16),
        "ffn_gate": w_ffn_gate.astype(BF16),
        "ffn_up": w_ffn_up.astype(BF16),
        "ffn_down": w_ffn_down.astype(BF16),
    }


def _layer(x2, ctx2, cv, tabs, layer, sw, w_ada, b_ada, g_mix, sink_a, g_q_a, g_kv_a, rpb_c, g_ffn, last):
    bsz, seq, ctx, d = BATCH, SEQ, CTX_LEN, D_MODEL
    cos_a, sin_a, cos_b, sin_b = tabs
    kvb = A_KV_HEADS * HEAD_DIM // LANE
    chb = C_HEADS * HEAD_DIM // LANE
    na_hb = math.gcd(chb, 8)

    mod = _adaln(cv, w_ada, b_ada, layer)
    sh_m, sc_m, gt_m, sh_f, sc_f, gt_f = [mod[:bsz, i * d:(i + 1) * d].reshape(bsz, 1, d) for i in range(6)]
    csh_m, csc_m, cgt_m, csh_f, csc_f, cgt_f = [mod[bsz:bsz + 1, i * d:(i + 1) * d].reshape(1, 1, d) for i in range(6)]
    w = {name: _LayerWeight(arr, layer) for name, arr in sw.items() if name != "br"}

    cm = bsz * ctx
    hc = _norm_mod(ctx2, g_mix, csh_m, csc_m, cm)
    ropecols_c = _mm_plain(hc, w["rope"] if not last else w["rope"].cols(kvb * LANE), bm=cm, name="mm_plain_c")
    plain_c = _mm_plain(hc, w["plain"], bm=cm, name="mm_plain_c")
    ckv_c, kr_c = _mm_ckv(hc, w["ckv"], g_kv_a, None, None, cm, rope=False, bm=cm)

    hx = _norm_mod(x2, g_mix, sh_m, sc_m, seq)
    qscale = jnp.full((1, A_HEADS * HEAD_DIM), HEAD_LOGIT_SCALE, F32)
    cs_rope = jnp.concatenate([jnp.ones((1, kvb * LANE), F32), qscale], axis=1)
    cs_plain = jnp.concatenate([jnp.ones((1, 2 * chb * LANE), F32), jnp.full((1, chb * LANE), HEAD_LOGIT_SCALE, F32),
                                jnp.ones((1, kvb * LANE), F32)], axis=1)
    qk_a = _mm_rope(hx, w["rope"], cs_rope, cos_a, sin_a, seq)
    plain = _mm_plain(hx, w["plain"], cs_plain)
    ckv, kr = _mm_ckv(hx, w["ckv"], g_kv_a, cos_b, sin_b, seq, rope=True)
    bqa = _mm_rms(hx, w["bqa"], g_q_a)
    gates = _mm_sigmoid(hx, w["gate"])
    qb = _mm_qb(bqa, w["qb"], cos_b, sin_b, seq, rope=True)

    lk = ctx + seq
    ckv_all = jnp.concatenate([ckv_c.reshape(bsz, ctx, -1), ckv.reshape(bsz, seq, -1)], axis=1).reshape(bsz * lk, -1)
    kr_all = jnp.concatenate([kr_c.reshape(bsz, ctx, -1), kr.reshape(bsz, seq, -1)], axis=1).reshape(bsz * lk, -1)
    k_exp = _mm_kexp(ckv_all, w["uk"], kr_all).reshape(bsz, lk, -1)
    v_exp = _mm_vexp(ckv_all, w["uv"]).reshape(bsz, lk, -1)

    o_a = _window_attention(sink_a, qk_a, plain, ropecols_c, plain_c, bsz, seq, kvb, 3 * chb)
    o_b = _dense_attention(qb.reshape(bsz, seq, -1), k_exp, v_exp, heads=B_HEADS, groups=1, dk=MLA_QK, dv=V_DIM,
                           q_off=0, k_off=0, v_off=0, lk=lk, tq=1024, tk=_mla_chunk(lk), scale=None, ones_col=True,
                           name="mla_attn").reshape(bsz * seq, -1)
    o_c = _neighborhood_attention(plain, plain_c, rpb_c, bsz, seq, q_blk=2 * chb // na_hb, k_blk=0,
                                  v_blk=chb // na_hb, heads_per_step=na_hb)

    y = _mm_merge(o_a, o_b, o_c, sw["br"], layer, gates)
    x2 = _mm_resid(y, w["out"], x2, gt_m, seq)
    h2 = _norm_mod(x2, g_ffn, sh_f, sc_f, seq)
    act = _mm_ffn_up(h2, w["ffn_gate"], w["ffn_up"])
    x2 = _mm_resid(act, w["ffn_down"], x2, gt_f, seq, bm=512, bn=512, name="mm_ffn_down")

    if not last:
        gates_c = _mm_sigmoid(hc, w["gate"], bm=cm)
        bqa_c = _mm_rms(hc, w["bqa"], g_q_a, bm=cm)
        qb_c = _mm_qb(bqa_c, w["qb"], None, None, cm, rope=False, bm=cm)
        rc3 = ropecols_c.reshape(bsz, ctx, -1)
        pc3 = plain_c.reshape(bsz, ctx, -1)
        oc_a = _dense_attention(rc3, rc3, pc3, heads=A_HEADS, groups=A_HEADS // A_KV_HEADS, dk=HEAD_DIM, dv=HEAD_DIM,
                                q_off=kvb, k_off=0, v_off=3 * chb, lk=ctx, tq=ctx, tk=ctx, scale=HEAD_DIM ** -0.5,
                                sink=sink_a, name="ctx_attn_a").reshape(bsz * ctx, -1)
        oc_b = _dense_attention(qb_c.reshape(bsz, ctx, -1), k_exp, v_exp, heads=B_HEADS, groups=1, dk=MLA_QK,
                                dv=V_DIM, q_off=0, k_off=0, v_off=0, lk=ctx, tq=ctx, tk=ctx, scale=None, ones_col=True,
                                name="ctx_attn_b").reshape(bsz * ctx, -1)
        oc_c = _dense_attention(pc3, pc3, pc3, heads=C_HEADS, groups=1, dk=HEAD_DIM, dv=HEAD_DIM,
                                q_off=2 * chb, k_off=0, v_off=chb, lk=ctx, tq=ctx, tk=ctx,
                                scale=HEAD_DIM ** -0.5, name="ctx_attn_c").reshape(bsz * ctx, -1)
        yc = _mm_merge(oc_a, oc_b, oc_c, sw["br"], layer, gates_c, bm=cm, bn=512)
        ctx2 = _mm_resid(yc, w["out"], ctx2, cgt_m, cm, bm=cm, name="mm_resid_c")
        hc2 = _norm_mod(ctx2, g_ffn, csh_f, csc_f, cm)
        act_c = _mm_ffn_up(hc2, w["ffn_gate"], w["ffn_up"], bm=cm)
        ctx2 = _mm_resid(act_c, w["ffn_down"], ctx2, cgt_f, cm, bm=cm, bn=256, name="mm_ffn_down_c")
    return x2, ctx2


def _mla_chunk(lk):
    for parts in range(40, 0, -1):
        if lk % parts == 0 and (lk // parts) % (2 * LANE) == 0:
            return lk // parts
    return lk


def _window_attention(sink, qk_a, plain, ropecols_c, plain_c, bsz, seq, kvb, v_lane_blk):
    nblk = seq // BLOCK
    groups = A_HEADS // A_KV_HEADS
    gw = groups * HEAD_DIM
    ctx = plain_c.shape[0] // bsz
    kvs = math.gcd(A_KV_HEADS, 4)
    kw = kvs * HEAD_DIM

    voff = v_lane_blk // kvs

    def blk(d, off):
        def imap(b, h, n):
            return (b * nblk + jnp.clip(n + d, 0, nblk - 1), off + h)
        return pl.BlockSpec((BLOCK, kw), imap)

    def qspec(j):
        return pl.BlockSpec((BLOCK, gw), lambda b, h, n: (b * nblk + n, kvb // groups + kvs * h + j))

    kxspec = pl.BlockSpec((ctx, kw), lambda b, h, n: (b, h))
    vxspec = pl.BlockSpec((ctx, kw), lambda b, h, n: (b, voff + h))
    mspec = pl.BlockSpec((None, BLOCK, 3 * BLOCK + ctx),
                         lambda b, h, n: (jnp.where(n == 0, 0, jnp.where(n == nblk - 1, 2, 1)), 0, 0))
    in_specs = ([pl.BlockSpec(memory_space=pltpu.SMEM)] + [qspec(j) for j in range(kvs)]
                + [blk(-1, 0), blk(0, 0), blk(1, 0), blk(-1, voff), blk(0, voff), blk(1, voff), kxspec, vxspec, mspec])
    return pl.pallas_call(
        functools.partial(_win_body, groups=groups, kv_per_step=kvs),
        grid=(bsz, A_KV_HEADS // kvs, nblk),
        in_specs=in_specs,
        out_specs=pl.BlockSpec((BLOCK, kvs * gw), lambda b, h, n: (b * nblk + n, h)),
        out_shape=jax.ShapeDtypeStruct((bsz * seq, A_HEADS * HEAD_DIM), BF16),
        compiler_params=_params(3),
        name="win_attn",
    )(sink, *([qk_a] * kvs), qk_a, qk_a, qk_a, plain, plain, plain, ropecols_c, plain_c,
      jnp.asarray(_win_masks(ctx)))


def kernel(x, c, ctx, c_ctx, w_ada, b_ada, g_mix, w_in, sink_a, g_q_a, w_q_b, g_kv_a, w_kv_b, rpb_c, w_br, w_out,
           g_ffn, w_ffn_gate, w_ffn_up, w_ffn_down, g_final):
    bsz, seq, d = x.shape
    tabs = _rope_tables(seq)
    pad = (-(bsz + 1)) % 8
    cv = jnp.concatenate([c, c_ctx[None, :], jnp.zeros((pad, d), F32)], axis=0)
    x2 = x.reshape(bsz * seq, d)
    ctx2 = ctx.reshape(bsz * ctx.shape[1], d)
    sw = _stacked_weights(w_in, w_q_b, w_kv_b, w_br, w_out, w_ffn_gate, w_ffn_up, w_ffn_down)
    for l in range(DEPTH):
        x2, ctx2 = _layer(x2, ctx2, cv, tabs, l, sw, w_ada, b_ada, g_mix[l], sink_a[l], g_q_a[l], g_kv_a[l], rpb_c[l],
                          g_ffn[l], last=(l == DEPTH - 1))
    zero = jnp.zeros((1, 1, d), F32)
    out = _norm_mod(x2, g_final, zero, zero, seq, out_dtype=F32)
    return out.reshape(bsz, seq, d)
```

```python
import functools
import math

import numpy as np
import jax
import jax.numpy as jnp
from jax import lax
from jax.experimental import pallas as pl
from jax.experimental.pallas import tpu as pltpu

D_MODEL = 4096
BATCH = 2
SEQ = 8192
DEPTH = 2
GRID_W = 64
CTX_LEN = 256
HEAD_DIM = 128
A_HEADS = 16
A_KV_HEADS = 4
WINDOW = 128
BLOCK = 128
B_HEADS = 16
Q_LORA = 1024
KV_LORA = 512
NOPE_DIM = 128
ROPE_DIM = 64
V_DIM = 128
C_HEADS = 16
NA_ROWS = 8
NA_COLS = 16
N_BRANCH = 3
ROPE_BASE = 10000.0
EPS = 1e-6
NEG_INF = -1e30

LANE = 128
MLA_QK = 2 * LANE
LOG2E = math.log2(math.e)
MLA_LOGIT_SCALE = (NOPE_DIM + ROPE_DIM) ** -0.5 * LOG2E
HEAD_LOGIT_SCALE = HEAD_DIM ** -0.5 * LOG2E
NA_QROWS = 4
VMEM_LIMIT = 56 * 1024 * 1024

BF16 = jnp.bfloat16
F32 = jnp.float32


def _params(n_axes):
    return pltpu.CompilerParams(dimension_semantics=("parallel",) * n_axes,
                                vmem_limit_bytes=VMEM_LIMIT)


def _dot(a, b):
    return jnp.dot(a, b, preferred_element_type=F32)


def _dot_nt(a, b):
    return lax.dot_general(a, b, (((1,), (1,)), ((), ())), preferred_element_type=F32)


def _tile(n, pref):
    if n <= pref:
        return n
    t = pref - pref % LANE
    while t >= LANE:
        if n % t == 0:
            return t
        t -= LANE
    return n


def _adaln_body(c_ref, w_ref, b_ref, o_ref):
    cv = c_ref[...]
    o_ref[...] = _dot(cv * jax.nn.sigmoid(cv), w_ref[...]) + b_ref[...]


def _adaln(cv, w_ada, b_ada, layer):
    rows, d = cv.shape
    n = w_ada.shape[2]
    bn = _tile(n, 512)
    return pl.pallas_call(
        _adaln_body,
        grid=(n // bn,),
        in_specs=[pl.BlockSpec((rows, d), lambda j: (0, 0)),
                  pl.BlockSpec((None, d, bn), lambda j: (layer, 0, j)),
                  pl.BlockSpec((None, 1, bn), lambda j: (layer, 0, j))],
        out_specs=pl.BlockSpec((rows, bn), lambda j: (0, j)),
        out_shape=jax.ShapeDtypeStruct((rows, n), F32),
        compiler_params=_params(1),
        name="adaln",
    )(cv, w_ada, b_ada.reshape(b_ada.shape[0], 1, n))


def _norm_mod_body(x_ref, g_ref, sh_ref, sc_ref, o_ref):
    x = x_ref[...]
    y = x * lax.rsqrt(jnp.mean(x * x, axis=-1, keepdims=True) + EPS) * g_ref[...]
    o_ref[...] = (y * (1.0 + sc_ref[...]) + sh_ref[...]).astype(o_ref.dtype)


def _norm_mod(x2, g, shift, scale, rows_per_batch, out_dtype=BF16):
    m, d = x2.shape
    ts = _tile(rows_per_batch, 512)
    tpb = rows_per_batch // ts
    per_batch = shift.shape[0] > 1
    mod_spec = pl.BlockSpec((None, 1, d), (lambda i: (i // tpb, 0, 0)) if per_batch else (lambda i: (0, 0, 0)))
    return pl.pallas_call(
        _norm_mod_body,
        grid=(m // ts,),
        in_specs=[pl.BlockSpec((ts, d), lambda i: (i, 0)),
                  pl.BlockSpec((1, d), lambda i: (0, 0)),
                  mod_spec, mod_spec],
        out_specs=pl.BlockSpec((ts, d), lambda i: (i, 0)),
        out_shape=jax.ShapeDtypeStruct((m, d), out_dtype),
        compiler_params=_params(1),
        name="norm_mod",
    )(x2, g.reshape(1, d), shift, scale)


def _a_spec(bm, k):
    return pl.BlockSpec((bm, k), lambda i, j: (i, 0))


class _LayerWeight:
    def __init__(self, arr, layer, ncols=None):
        self.arr, self.layer = arr, layer
        self.shape = (arr.shape[1], arr.shape[2] if ncols is None else ncols)

    def cols(self, ncols):
        return _LayerWeight(self.arr, self.layer, ncols)


def _w_spec(w, bn):
    return pl.BlockSpec((None, w.shape[0], bn), lambda i, j: (w.layer, 0, j))


def _mn_spec(bm, bn):
    return pl.BlockSpec((bm, bn), lambda i, j: (i, j))


def _pos_spec(bm, width, tiles_per_batch):
    return pl.BlockSpec((bm, width), lambda i, j: (i % tiles_per_batch, 0))


def _mm_call(body, grid, in_specs, out_specs, out_shape, args, name):
    args = [a.arr if isinstance(a, _LayerWeight) else a for a in args]
    return pl.pallas_call(body, grid=grid, in_specs=in_specs, out_specs=out_specs, out_shape=out_shape,
                          compiler_params=_params(2), name=name)(*args)


def _plain_body(a_ref, w_ref, o_ref):
    o_ref[...] = _dot(a_ref[...], w_ref[...]).astype(o_ref.dtype)


def _scaled_body(a_ref, w_ref, cs_ref, o_ref):
    o_ref[...] = (_dot(a_ref[...], w_ref[...]) * cs_ref[...]).astype(o_ref.dtype)


def _n_spec(bn):
    return pl.BlockSpec((1, bn), lambda i, j: (0, j))


def _mm_plain(a, w, colscale=None, bm=2048, bn=512, name="mm_plain"):
    m, k = a.shape
    n = w.shape[1]
    bm, bn = _tile(m, bm), _tile(n, bn)
    if colscale is None:
        return _mm_call(_plain_body, (m // bm, n // bn), [_a_spec(bm, k), _w_spec(w, bn)], _mn_spec(bm, bn),
                        jax.ShapeDtypeStruct((m, n), BF16), (a, w), name)
    return _mm_call(_scaled_body, (m // bm, n // bn), [_a_spec(bm, k), _w_spec(w, bn), _n_spec(bn)],
                    _mn_spec(bm, bn), jax.ShapeDtypeStruct((m, n), BF16), (a, w, colscale), name)


def _rope_chunks(acc, cos, sin):
    outs = []
    for c in range(acc.shape[1] // LANE):
        t = acc[:, c * LANE:(c + 1) * LANE]
        outs.append(t * cos + pltpu.roll(t, LANE // 2, 1) * sin)
    return outs[0] if len(outs) == 1 else jnp.concatenate(outs, axis=1)


def _rope_body(a_ref, w_ref, cs_ref, cos_ref, sin_ref, o_ref):
    acc = _dot(a_ref[...], w_ref[...]) * cs_ref[...]
    o_ref[...] = _rope_chunks(acc, cos_ref[...], sin_ref[...]).astype(o_ref.dtype)


def _mm_rope(a, w, colscale, cos2, sin2, rows_per_batch, bm=2048, bn=512):
    m, k = a.shape
    n = w.shape[1]
    bm, bn = _tile(rows_per_batch, bm), _tile(n, bn)
    tpb = rows_per_batch // bm
    return _mm_call(_rope_body, (m // bm, n // bn),
                    [_a_spec(bm, k), _w_spec(w, bn), _n_spec(bn), _pos_spec(bm, LANE, tpb), _pos_spec(bm, LANE, tpb)],
                    _mn_spec(bm, bn), jax.ShapeDtypeStruct((m, n), BF16), (a, w, colscale, cos2, sin2), "mm_rope")


def _ckv_body(a_ref, w_ref, g_ref, *rest, rope):
    if rope:
        cos_ref, sin_ref, ckv_ref, kr_ref = rest
    else:
        ckv_ref, kr_ref = rest
    acc = _dot(a_ref[...], w_ref[...])
    nl = ckv_ref.shape[1]
    t = acc[:, :nl]
    ckv_ref[...] = (t * lax.rsqrt(jnp.mean(t * t, axis=-1, keepdims=True) + EPS) * g_ref[...]).astype(ckv_ref.dtype)
    r = acc[:, nl:]
    if rope:
        r = _rope_chunks(r, cos_ref[...], sin_ref[...])
    kr_ref[...] = r.astype(kr_ref.dtype)


def _mm_ckv(a, w, g, cos2, sin2, rows_per_batch, rope, bm=1024):
    m, k = a.shape
    n = w.shape[1]
    nl = n - LANE
    bm = _tile(rows_per_batch, bm)
    tpb = rows_per_batch // bm
    in_specs = [_a_spec(bm, k), _w_spec(w, n), pl.BlockSpec((1, nl), lambda i, j: (0, 0))]
    args = [a, w, g.reshape(1, nl)]
    if rope:
        in_specs += [_pos_spec(bm, LANE, tpb), _pos_spec(bm, LANE, tpb)]
        args += [cos2, sin2]
    return _mm_call(functools.partial(_ckv_body, rope=rope), (m // bm, 1), in_specs,
                    [pl.BlockSpec((bm, nl), lambda i, j: (i, 0)), pl.BlockSpec((bm, LANE), lambda i, j: (i, 0))],
                    [jax.ShapeDtypeStruct((m, nl), BF16), jax.ShapeDtypeStruct((m, LANE), BF16)], args, "mm_ckv")


def _rms_body(a_ref, w_ref, g_ref, o_ref):
    t = _dot(a_ref[...], w_ref[...])
    o_ref[...] = (t * lax.rsqrt(jnp.mean(t * t, axis=-1, keepdims=True) + EPS) * g_ref[...]).astype(o_ref.dtype)


def _mm_rms(a, w, g, bm=1024):
    m, k = a.shape
    n = w.shape[1]
    bm = _tile(m, bm)
    return _mm_call(_rms_body, (m // bm, 1), [_a_spec(bm, k), _w_spec(w, n), pl.BlockSpec((1, n), lambda i, j: (0, 0))],
                    _mn_spec(bm, n), jax.ShapeDtypeStruct((m, n), BF16), (a, w, g.reshape(1, n)), "mm_rms")


def _sigmoid_body(a_ref, w_ref, o_ref):
    o_ref[...] = jax.nn.sigmoid(_dot(a_ref[...], w_ref[...]))


def _mm_sigmoid(a, w, bm=1024, bn=1024):
    m, k = a.shape
    n = w.shape[1]
    bm, bn = _tile(m, bm), _tile(n, bn)
    return _mm_call(_sigmoid_body, (m // bm, n // bn), [_a_spec(bm, k), _w_spec(w, bn)], _mn_spec(bm, bn),
                    jax.ShapeDtypeStruct((m, n), F32), (a, w), "mm_gates")


def _qb_body(a_ref, w_ref, *rest, rope):
    if rope:
        cos_ref, sin_ref, o_ref = rest
    else:
        (o_ref,) = rest
    acc = _dot(a_ref[...], w_ref[...]) * MLA_LOGIT_SCALE
    if rope:
        parts = []
        for h in range(acc.shape[1] // MLA_QK):
            parts.append(acc[:, h * MLA_QK:h * MLA_QK + NOPE_DIM])
            parts.append(_rope_chunks(acc[:, h * MLA_QK + NOPE_DIM:(h + 1) * MLA_QK], cos_ref[...], sin_ref[...]))
        acc = jnp.concatenate(parts, axis=1)
    o_ref[...] = acc.astype(o_ref.dtype)


def _mm_qb(a, w, cos2, sin2, rows_per_batch, rope, bm=1024):
    m, k = a.shape
    n = w.shape[1]
    bm = _tile(rows_per_batch, bm)
    bn = 4 * MLA_QK if n % (4 * MLA_QK) == 0 else MLA_QK
    tpb = rows_per_batch // bm
    in_specs = [_a_spec(bm, k), _w_spec(w, bn)]
    args = [a, w]
    if rope:
        in_specs += [_pos_spec(bm, LANE, tpb), _pos_spec(bm, LANE, tpb)]
        args += [cos2, sin2]
    return _mm_call(functools.partial(_qb_body, rope=rope), (m // bm, n // bn), in_specs, _mn_spec(bm, bn),
                    jax.ShapeDtypeStruct((m, n), BF16), args, "mm_qb")


def _kexp_body(a_ref, w_ref, kr_ref, o_ref):
    acc = _dot(a_ref[...], w_ref[...]).astype(o_ref.dtype)
    kr = kr_ref[...]
    parts = []
    for h in range(acc.shape[1] // NOPE_DIM):
        parts += [acc[:, h * NOPE_DIM:(h + 1) * NOPE_DIM], kr]
    o_ref[...] = jnp.concatenate(parts, axis=1)


def _mm_kexp(ckv, w_uk, kr, bm=1536, heads_per_tile=8):
    m, k = ckv.shape
    n = w_uk.shape[1]
    heads = n // NOPE_DIM
    bm = _tile(m, bm)
    hpt = heads_per_tile if heads % heads_per_tile == 0 else 1
    return _mm_call(_kexp_body, (m // bm, heads // hpt),
                    [_a_spec(bm, k), _w_spec(w_uk, hpt * NOPE_DIM), pl.BlockSpec((bm, LANE), lambda i, j: (i, 0))],
                    _mn_spec(bm, hpt * MLA_QK), jax.ShapeDtypeStruct((m, heads * MLA_QK), BF16),
                    (ckv, w_uk, kr), "mm_kexp")


def _vexp_body(a_ref, w_ref, o_ref):
    acc = _dot(a_ref[...], w_ref[...]).astype(o_ref.dtype)
    lane = lax.broadcasted_iota(jnp.int32, (acc.shape[0], V_DIM), 1)
    ones = jnp.where(lane == 0, 1.0, 0.0).astype(o_ref.dtype)
    parts = []
    for h in range(acc.shape[1] // V_DIM):
        parts += [acc[:, h * V_DIM:(h + 1) * V_DIM], ones]
    o_ref[...] = jnp.concatenate(parts, axis=1)


def _mm_vexp(ckv, w_uv, bm=1536, heads_per_tile=8):
    m, k = ckv.shape
    heads = w_uv.shape[1] // V_DIM
    bm = _tile(m, bm)
    hpt = heads_per_tile if heads % heads_per_tile == 0 else 1
    return _mm_call(_vexp_body, (m // bm, heads // hpt), [_a_spec(bm, k), _w_spec(w_uv, hpt * V_DIM)],
                    _mn_spec(bm, 2 * hpt * V_DIM), jax.ShapeDtypeStruct((m, 2 * heads * V_DIM), BF16),
                    (ckv, w_uv), "mm_vexp")


def _merge_body(oa_ref, ob_ref, oc_ref, wa_ref, wb_ref, wc_ref, ga_ref, gb_ref, gc_ref, o_ref):
    y = ga_ref[...] * _dot(oa_ref[...], wa_ref[...])
    y = y + gb_ref[...] * _dot(ob_ref[...], wb_ref[...])
    y = y + gc_ref[...] * _dot(oc_ref[...], wc_ref[...])
    o_ref[...] = y.astype(o_ref.dtype)


def _mm_merge(o_a, o_b, o_c, w_br, layer, gates, bm=1024, bn=256):
    m, k = o_a.shape
    n = w_br.shape[3]
    bm, bn = _tile(m, bm), _tile(n, bn)
    nb = n // bn
    wspec = [pl.BlockSpec((None, None, k, bn), (lambda i, j, r=r: (layer, r, 0, j))) for r in range(N_BRANCH)]
    gspec = [pl.BlockSpec((bm, bn), (lambda i, j, r=r: (i, r * nb + j))) for r in range(N_BRANCH)]
    return _mm_call(_merge_body, (m // bm, nb), [_a_spec(bm, k)] * 3 + wspec + gspec, _mn_spec(bm, bn),
                    jax.ShapeDtypeStruct((m, n), BF16), (o_a, o_b, o_c, w_br, w_br, w_br, gates, gates, gates),
                    "mm_merge")


def _resid_body(a_ref, w_ref, x_ref, gt_ref, o_ref):
    o_ref[...] = x_ref[...] + gt_ref[...] * _dot(a_ref[...], w_ref[...])


def _mm_resid(a, w, x2, gate, rows_per_batch, bm=1024, bn=512, name="mm_resid"):
    m, k = a.shape
    n = w.shape[1]
    bm, bn = _tile(rows_per_batch, bm), _tile(n, bn)
    tpb = rows_per_batch // bm
    per_batch = gate.shape[0] > 1
    gspec = pl.BlockSpec((None, 1, bn), (lambda i, j: (i // tpb, 0, j)) if per_batch else (lambda i, j: (0, 0, j)))
    return _mm_call(_resid_body, (m // bm, n // bn), [_a_spec(bm, k), _w_spec(w, bn), _mn_spec(bm, bn), gspec],
                    _mn_spec(bm, bn), jax.ShapeDtypeStruct((m, n), F32), (a, w, x2, gate), name)


def _ffn_up_body(a_ref, wg_ref, wu_ref, o_ref):
    a = a_ref[...]
    g = _dot(a, wg_ref[...])
    o_ref[...] = (g * jax.nn.sigmoid(g) * _dot(a, wu_ref[...])).astype(o_ref.dtype)


def _mm_ffn_up(a, wg, wu, bm=2048, bn=256):
    m, k = a.shape
    n = wg.shape[1]
    bm, bn = _tile(m, bm), _tile(n, bn)
    return _mm_call(_ffn_up_body, (m // bm, n // bn), [_a_spec(bm, k), _w_spec(wg, bn), _w_spec(wu, bn)],
                    _mn_spec(bm, bn), jax.ShapeDtypeStruct((m, n), BF16), (a, wg, wu), "mm_ffn_up")


def _with_ones_col(v):
    lane = lax.broadcasted_iota(jnp.int32, v.shape, 1)
    return jnp.concatenate([v, jnp.where(lane == 0, 1.0, 0.0).astype(v.dtype)], axis=1)


def _win_body(sink_ref, *refs, groups, kv_per_step):
    q_refs = refs[:kv_per_step]
    kp_ref, kc_ref, kn_ref, vp_ref, vc_ref, vn_ref, kx_ref, vx_ref, mask_ref, o_ref = refs[kv_per_step:]
    kv0 = pl.program_id(1) * kv_per_step
    mask = mask_ref[...]
    chains = [(j, g) for j in range(kv_per_step) for g in range(groups)]
    k_all, v_all = [], []
    for j in range(kv_per_step):
        c = slice(j * HEAD_DIM, (j + 1) * HEAD_DIM)
        k_all.append(jnp.concatenate([kp_ref[:, c], kc_ref[:, c], kn_ref[:, c], kx_ref[:, c]], axis=0))
        v_all.append(_with_ones_col(jnp.concatenate([vp_ref[:, c], vc_ref[:, c], vn_ref[:, c], vx_ref[:, c]], axis=0)))
    s = [_dot_nt(q_refs[j][:, g * HEAD_DIM:(g + 1) * HEAD_DIM], k_all[j]) + mask for j, g in chains]
    sink = [sink_ref[(kv0 + j) * groups + g] * LOG2E for j, g in chains]
    m = [jnp.maximum(jnp.max(si, axis=1, keepdims=True), sk) for si, sk in zip(s, sink)]
    p = [jnp.exp2(si - mi).astype(BF16) for si, mi in zip(s, m)]
    for i, (j, g) in enumerate(chains):
        acc = _dot(p[i], v_all[j])
        l = acc[:, HEAD_DIM:HEAD_DIM + 1] + jnp.exp2(sink[i] - m[i])
        c0 = (j * groups + g) * HEAD_DIM
        o_ref[:, c0:c0 + HEAD_DIM] = (acc[:, :HEAD_DIM] / l).astype(o_ref.dtype)


def _win_masks(ctx):
    qi = np.arange(BLOCK)[:, None]
    kj = np.arange(3 * BLOCK)[None, :]
    band = np.abs(kj - qi - BLOCK) <= WINDOW
    out = []
    for lo, hi in ((BLOCK, 3 * BLOCK), (0, 3 * BLOCK), (0, 2 * BLOCK)):
        ok = band & (kj >= lo) & (kj < hi)
        out.append(np.concatenate([np.where(ok, 0.0, NEG_INF), np.zeros((BLOCK, ctx))], axis=1))
    return np.stack(out).astype(np.float32)


def _dense_body(*refs, tk, scale, has_sink, ones_col):
    if has_sink:
        sink_ref, q_ref, k_ref, v_ref, o_ref = refs
    else:
        q_ref, k_ref, v_ref, o_ref = refs
    q = q_ref[...]
    lk = k_ref.shape[0]
    dv = o_ref.shape[1]
    exp = jnp.exp2 if scale is None else jnp.exp
    m = l = acc = None
    nchunks = lk // tk

    def logits(c):
        s = _dot_nt(q, k_ref[c * tk:(c + 1) * tk, :])
        return s if scale is None else s * scale

    for c in range(nchunks):
        s = logits(c)
        mc = jnp.max(s, axis=1, keepdims=True)
        if c == 0:
            m_new = mc
            if has_sink:
                sink = sink_ref[pl.program_id(1)]
                m_new = jnp.maximum(m_new, sink)
        else:
            m_new = jnp.maximum(m, mc)
        p = exp(s - m_new)
        pv = _dot(p.astype(BF16), v_ref[c * tk:(c + 1) * tk, :])
        if not ones_col:
            ps = jnp.sum(p, axis=1, keepdims=True)
            if c == 0:
                l = ps + exp(sink - m_new) if has_sink else ps
            else:
                l = exp(m - m_new) * l + ps
        acc = pv if c == 0 else exp(m - m_new) * acc + pv
        m = m_new
    if ones_col:
        l = acc[:, dv:dv + 1]
        acc = acc[:, :dv]
    o_ref[...] = (acc / l).astype(o_ref.dtype)


def _dense_attention(q3, k3, v3, *, heads, groups, dk, dv, q_off, k_off, v_off, lk, tq, tk, scale, sink=None,
                     ones_col=False, name="dense_attn"):
    assert not (ones_col and sink is not None)
    bsz, lq, _ = q3.shape
    tq = _tile(lq, tq)
    dvw = 2 * dv if ones_col else dv
    in_specs = [pl.BlockSpec((None, tq, dk), lambda b, h, i: (b, i, q_off + h)),
                pl.BlockSpec((None, lk, dk), lambda b, h, i: (b, 0, k_off + h // groups)),
                pl.BlockSpec((None, lk, dvw), lambda b, h, i: (b, 0, v_off + h // groups))]
    args = [q3, k3, v3]
    if sink is not None:
        in_specs = [pl.BlockSpec(memory_space=pltpu.SMEM)] + in_specs
        args = [sink] + args
    return pl.pallas_call(
        functools.partial(_dense_body, tk=tk, scale=scale, has_sink=sink is not None, ones_col=ones_col),
        grid=(bsz, heads, lq // tq),
        in_specs=in_specs,
        out_specs=pl.BlockSpec((None, tq, dv), lambda b, h, i: (b, i, h)),
        out_shape=jax.ShapeDtypeStruct((bsz, lq, heads * dv), BF16),
        compiler_params=_params(3),
        name=name,
    )(*args)


def _na_body(q_ref, kp_ref, kc_ref, kn_ref, vp_ref, vc_ref, vn_ref, kx_ref, vx_ref, bias_ref, o_ref):
    heads = range(bias_ref.shape[0])
    cols = [slice(h * HEAD_DIM, (h + 1) * HEAD_DIM) for h in heads]
    s = [_dot_nt(q_ref[:, c], jnp.concatenate([kp_ref[:, c], kc_ref[:, c], kn_ref[:, c], kx_ref[:, c]], axis=0))
         + bias_ref[h] for h, c in zip(heads, cols)]
    p = [jnp.exp2(sh - jnp.max(sh, axis=1, keepdims=True)).astype(BF16) for sh in s]
    for h, c in zip(heads, cols):
        v_all = _with_ones_col(jnp.concatenate([vp_ref[:, c], vc_ref[:, c], vn_ref[:, c], vx_ref[:, c]], axis=0))
        acc = _dot(p[h], v_all)
        o_ref[:, c] = (acc[:, :HEAD_DIM] / acc[:, HEAD_DIM:HEAD_DIM + 1]).astype(o_ref.dtype)


def _na_valid(rows):
    r = NA_QROWS
    nblk = rows // r
    assert rows % r == 0 and nblk >= 3 and rows >= NA_ROWS and NA_ROWS >= 2 * r
    kwin = min(NA_ROWS, rows)
    valid = []
    for i in (0, 1, nblk - 1):
        qrow = (r * i + np.arange(r))[:, None, None, None]
        qcol = np.arange(GRID_W)[None, :, None, None]
        krow = (r * (i - 1) + np.arange(3 * r))[None, None, :, None]
        kcol = np.arange(GRID_W)[None, None, None, :]
        r0 = np.clip(qrow - kwin // 2, 0, rows - kwin)
        c0 = np.clip(qcol - NA_COLS // 2, 0, GRID_W - NA_COLS)
        ok = (krow >= 0) & (krow < rows) & (krow >= r0) & (krow < r0 + kwin) & (kcol >= c0) & (kcol < c0 + NA_COLS)
        valid.append(ok.reshape(r * GRID_W, 3 * r * GRID_W))
    return np.stack(valid)


def _na_bias_tables(rpb, rows, ctx):
    r, w = NA_QROWS, GRID_W
    heads, nro, nco = rpb.shape
    circ = jnp.concatenate([rpb[..., NA_COLS - 1:], jnp.zeros((heads, nro, 2 * w - nco), rpb.dtype),
                            rpb[..., :NA_COLS - 1]], axis=-1)
    toep = jnp.broadcast_to(circ[:, :, None, :], (heads, nro, w, 2 * w)).reshape(heads, nro, 2 * w * w)
    toep = toep[:, :, :w * (2 * w - 1)].reshape(heads, nro, w, 2 * w - 1)[..., :w]
    strips = []
    for qr in range(r):
        lo = NA_ROWS - 1 - r - qr
        strips.append(toep[:, lo:lo + 3 * r].transpose(0, 2, 1, 3).reshape(heads, w, 3 * r * w))
    tile = jnp.concatenate(strips, axis=1) * LOG2E
    local = jnp.where(jnp.asarray(_na_valid(rows))[None], tile[:, None], NEG_INF)
    return jnp.concatenate([local, jnp.zeros(local.shape[:3] + (ctx,), local.dtype)], axis=-1)


def _neighborhood_attention(plain, plain_c, rpb, bsz, seq, q_blk, k_blk, v_blk, heads_per_step):
    rows = seq // GRID_W
    tq = NA_QROWS * GRID_W
    nblk = rows // NA_QROWS
    ctx = plain_c.shape[0] // bsz
    hb = heads_per_step
    hw = hb * HEAD_DIM
    bias = _na_bias_tables(rpb, rows, ctx)

    def blk(off, d):
        def imap(b, h, i):
            return (b * nblk + jnp.clip(i + d, 0, nblk - 1), off + h)
        return pl.BlockSpec((tq, hw), imap)

    def bias_map(b, h, i):
        return (h, jnp.where(i == 0, 0, jnp.where(i == nblk - 1, 2, 1)), 0, 0)

    in_specs = [blk(q_blk, 0),
                blk(k_blk, -1), blk(k_blk, 0), blk(k_blk, 1),
                blk(v_blk, -1), blk(v_blk, 0), blk(v_blk, 1),
                pl.BlockSpec((ctx, hw), lambda b, h, i: (b, k_blk + h)),
                pl.BlockSpec((ctx, hw), lambda b, h, i: (b, v_blk + h)),
                pl.BlockSpec((hb, None, tq, 3 * tq + ctx), bias_map)]
    return pl.pallas_call(
        _na_body,
        grid=(bsz, C_HEADS // hb, nblk),
        in_specs=in_specs,
        out_specs=pl.BlockSpec((tq, hw), lambda b, h, i: (b * nblk + i, h)),
        out_shape=jax.ShapeDtypeStruct((bsz * seq, C_HEADS * HEAD_DIM), BF16),
        compiler_params=_params(3),
        name="na_attn",
    )(plain, plain, plain, plain, plain, plain, plain, plain_c, plain_c, bias)


def _rope_tables(seq):
    t = jnp.arange(seq, dtype=jnp.int32)
    row = (t // GRID_W).astype(F32)
    col = (t % GRID_W).astype(F32)

    def cs(rot_dim):
        n_freq = rot_dim // 4
        inv = jnp.power(ROPE_BASE, -jnp.arange(n_freq, dtype=F32) / n_freq)
        ang = jnp.concatenate([row[:, None] * inv, col[:, None] * inv], axis=-1)
        return jnp.cos(ang), jnp.sin(ang)

    ca, sa = cs(HEAD_DIM)
    cb, sb = cs(ROPE_DIM)
    zb = jnp.zeros_like(cb)
    return (jnp.concatenate([ca, ca], axis=1), jnp.concatenate([-sa, sa], axis=1),
            jnp.concatenate([cb, zb, cb, zb], axis=1), jnp.concatenate([-sb, zb, sb, zb], axis=1))


def _spread_rope_cols(w):
    h = ROPE_DIM // 2
    z = jnp.zeros(w.shape[:-1] + (LANE // 2 - h,), w.dtype)
    return jnp.concatenate([w[..., :h], z, w[..., h:], z], axis=-1)


def _cast_body(x_ref, o_ref):
    o_ref[...] = x_ref[...].astype(o_ref.dtype)


def _cast_bf16(w):
    nl, k, n = w.shape
    tr = _tile(k, LANE)
    spec = pl.BlockSpec((None, tr, n), lambda l, i: (l, i, 0))
    return pl.pallas_call(_cast_body, grid=(nl, k // tr), in_specs=[spec], out_specs=spec,
                          out_shape=jax.ShapeDtypeStruct(w.shape, BF16), compiler_params=_params(2),
                          name="cast_bf16")(w)


def _stacked_weights(w_in, w_q_b, w_kv_b, w_br, w_out, w_ffn_gate, w_ffn_up, w_ffn_down):
    nl = w_in.shape[0]
    w_in = _cast_bf16(w_in)
    kvh = A_KV_HEADS * HEAD_DIM
    ch = C_HEADS * HEAD_DIM
    ah = A_HEADS * HEAD_DIM
    sizes = [kvh, kvh, KV_LORA, ROPE_DIM, ch, ch, ah, Q_LORA, ch, N_BRANCH * D_MODEL]
    offs = np.concatenate([[0], np.cumsum(sizes)])
    a_k, a_v, b_ckv, b_kr, c_k, c_v, a_q, b_qa, c_q, gate = [w_in[..., offs[i]:offs[i + 1]]
                                                             for i in range(len(sizes))]
    return {
        "rope": jnp.concatenate([a_k, a_q], axis=-1).astype(BF16),
        "plain": jnp.concatenate([c_k, c_v, c_q, a_v], axis=-1).astype(BF16),
        "ckv": jnp.concatenate([b_ckv, _spread_rope_cols(b_kr)], axis=-1).astype(BF16),
        "bqa": b_qa.astype(BF16),
        "gate": gate.astype(BF16),
        "qb": jnp.concatenate([w_q_b[..., :NOPE_DIM], _spread_rope_cols(w_q_b[..., NOPE_DIM:])], axis=-1)
        .reshape(nl, Q_LORA, B_HEADS * MLA_QK).astype(BF16),
        "uk": w_kv_b[..., :NOPE_DIM].reshape(nl, KV_LORA, B_HEADS * NOPE_DIM).astype(BF16),
        "uv": w_kv_b[..., NOPE_DIM:].reshape(nl, KV_LORA, B_HEADS * V_DIM).astype(BF16),
        "br": w_br.astype(BF16),
        "out": w_out.astype(BF16),
        "ffn_gate": w_ffn_gate.astype(BF16),
        "ffn_up": w_ffn_up.astype(BF16),
        "ffn_down": w_ffn_down.astype(BF16),
    }


def _layer(x2, ctx2, cv, tabs, layer, sw, w_ada, b_ada, g_mix, sink_a, g_q_a, g_kv_a, rpb_c, g_ffn, last):
    bsz, seq, ctx, d = BATCH, SEQ, CTX_LEN, D_MODEL
    cos_a, sin_a, cos_b, sin_b = tabs
    kvb = A_KV_HEADS * HEAD_DIM // LANE
    chb = C_HEADS * HEAD_DIM // LANE
    na_hb = math.gcd(chb, 8)

    mod = _adaln(cv, w_ada, b_ada, layer)
    sh_m, sc_m, gt_m, sh_f, sc_f, gt_f = [mod[:bsz, i * d:(i + 1) * d].reshape(bsz, 1, d) for i in range(6)]
    csh_m, csc_m, cgt_m, csh_f, csc_f, cgt_f = [mod[bsz:bsz + 1, i * d:(i + 1) * d].reshape(1, 1, d) for i in range(6)]
    w = {name: _LayerWeight(arr, layer) for name, arr in sw.items() if name != "br"}

    cm = bsz * ctx
    hc = _norm_mod(ctx2, g_mix, csh_m, csc_m, cm)
    ropecols_c = _mm_plain(hc, w["rope"] if not last else w["rope"].cols(kvb * LANE), bm=cm, name="mm_plain_c")
    plain_c = _mm_plain(hc, w["plain"], bm=cm, name="mm_plain_c")
    ckv_c, kr_c = _mm_ckv(hc, w["ckv"], g_kv_a, None, None, cm, rope=False, bm=cm)

    hx = _norm_mod(x2, g_mix, sh_m, sc_m, seq)
    qscale = jnp.full((1, A_HEADS * HEAD_DIM), HEAD_LOGIT_SCALE, F32)
    cs_rope = jnp.concatenate([jnp.ones((1, kvb * LANE), F32), qscale], axis=1)
    cs_plain = jnp.concatenate([jnp.ones((1, 2 * chb * LANE), F32), jnp.full((1, chb * LANE), HEAD_LOGIT_SCALE, F32),
                                jnp.ones((1, kvb * LANE), F32)], axis=1)
    qk_a = _mm_rope(hx, w["rope"], cs_rope, cos_a, sin_a, seq)
    plain = _mm_plain(hx, w["plain"], cs_plain)
    ckv, kr = _mm_ckv(hx, w["ckv"], g_kv_a, cos_b, sin_b, seq, rope=True)
    bqa = _mm_rms(hx, w["bqa"], g_q_a)
    gates = _mm_sigmoid(hx, w["gate"])
    qb = _mm_qb(bqa, w["qb"], cos_b, sin_b, seq, rope=True)

    lk = ctx + seq
    ckv_all = jnp.concatenate([ckv_c.reshape(bsz, ctx, -1), ckv.reshape(bsz, seq, -1)], axis=1).reshape(bsz * lk, -1)
    kr_all = jnp.concatenate([kr_c.reshape(bsz, ctx, -1), kr.reshape(bsz, seq, -1)], axis=1).reshape(bsz * lk, -1)
    k_exp = _mm_kexp(ckv_all, w["uk"], kr_all).reshape(bsz, lk, -1)
    v_exp = _mm_vexp(ckv_all, w["uv"]).reshape(bsz, lk, -1)

    o_a = _window_attention(sink_a, qk_a, plain, ropecols_c, plain_c, bsz, seq, kvb, 3 * chb)
    o_b = _dense_attention(qb.reshape(bsz, seq, -1), k_exp, v_exp, heads=B_HEADS, groups=1, dk=MLA_QK, dv=V_DIM,
                           q_off=0, k_off=0, v_off=0, lk=lk, tq=1024, tk=_mla_chunk(lk), scale=None, ones_col=True,
                           name="mla_attn").reshape(bsz * seq, -1)
    o_c = _neighborhood_attention(plain, plain_c, rpb_c, bsz, seq, q_blk=2 * chb // na_hb, k_blk=0,
                                  v_blk=chb // na_hb, heads_per_step=na_hb)

    y = _mm_merge(o_a, o_b, o_c, sw["br"], layer, gates)
    x2 = _mm_resid(y, w["out"], x2, gt_m, seq)
    h2 = _norm_mod(x2, g_ffn, sh_f, sc_f, seq)
    act = _mm_ffn_up(h2, w["ffn_gate"], w["ffn_up"])
    x2 = _mm_resid(act, w["ffn_down"], x2, gt_f, seq, bm=512, bn=512, name="mm_ffn_down")

    if not last:
        gates_c = _mm_sigmoid(hc, w["gate"], bm=cm)
        bqa_c = _mm_rms(hc, w["bqa"], g_q_a, bm=cm)
        qb_c = _mm_qb(bqa_c, w["qb"], None, None, cm, rope=False, bm=cm)
        rc3 = ropecols_c.reshape(bsz, ctx, -1)
        pc3 = plain_c.reshape(bsz, ctx, -1)
        oc_a = _dense_attention(rc3, rc3, pc3, heads=A_HEADS, groups=A_HEADS // A_KV_HEADS, dk=HEAD_DIM, dv=HEAD_DIM,
                                q_off=kvb, k_off=0, v_off=3 * chb, lk=ctx, tq=ctx, tk=ctx, scale=HEAD_DIM ** -0.5,
                                sink=sink_a, name="ctx_attn_a").reshape(bsz * ctx, -1)
        oc_b = _dense_attention(qb_c.reshape(bsz, ctx, -1), k_exp, v_exp, heads=B_HEADS, groups=1, dk=MLA_QK,
                                dv=V_DIM, q_off=0, k_off=0, v_off=0, lk=ctx, tq=ctx, tk=ctx, scale=None, ones_col=True,
                                name="ctx_attn_b").reshape(bsz * ctx, -1)
        oc_c = _dense_attention(pc3, pc3, pc3, heads=C_HEADS, groups=1, dk=HEAD_DIM, dv=HEAD_DIM,
                                q_off=2 * chb, k_off=0, v_off=chb, lk=ctx, tq=ctx, tk=ctx,
                                scale=HEAD_DIM ** -0.5, name="ctx_attn_c").reshape(bsz * ctx, -1)
        yc = _mm_merge(oc_a, oc_b, oc_c, sw["br"], layer, gates_c, bm=cm, bn=512)
        ctx2 = _mm_resid(yc, w["out"], ctx2, cgt_m, cm, bm=cm, name="mm_resid_c")
        hc2 = _norm_mod(ctx2, g_ffn, csh_f, csc_f, cm)
        act_c = _mm_ffn_up(hc2, w["ffn_gate"], w["ffn_up"], bm=cm)
        ctx2 = _mm_resid(act_c, w["ffn_down"], ctx2, cgt_f, cm, bm=cm, bn=256, name="mm_ffn_down_c")
    return x2, ctx2


def _mla_chunk(lk):
    for parts in range(40, 0, -1):
        if lk % parts == 0 and (lk // parts) % (2 * LANE) == 0:
            return lk // parts
    return lk


def _window_attention(sink, qk_a, plain, ropecols_c, plain_c, bsz, seq, kvb, v_lane_blk):
    nblk = seq // BLOCK
    groups = A_HEADS // A_KV_HEADS
    gw = groups * HEAD_DIM
    ctx = plain_c.shape[0] // bsz
    kvs = math.gcd(A_KV_HEADS, 4)
    kw = kvs * HEAD_DIM

    voff = v_lane_blk // kvs

    def blk(d, off):
        def imap(b, h, n):
            return (b * nblk + jnp.clip(n + d, 0, nblk - 1), off + h)
        return pl.BlockSpec((BLOCK, kw), imap)

    def qspec(j):
        return pl.BlockSpec((BLOCK, gw), lambda b, h, n: (b * nblk + n, kvb // groups + kvs * h + j))

    kxspec = pl.BlockSpec((ctx, kw), lambda b, h, n: (b, h))
    vxspec = pl.BlockSpec((ctx, kw), lambda b, h, n: (b, voff + h))
    mspec = pl.BlockSpec((None, BLOCK, 3 * BLOCK + ctx),
                         lambda b, h, n: (jnp.where(n == 0, 0, jnp.where(n == nblk - 1, 2, 1)), 0, 0))
    in_specs = ([pl.BlockSpec(memory_space=pltpu.SMEM)] + [qspec(j) for j in range(kvs)]
                + [blk(-1, 0), blk(0, 0), blk(1, 0), blk(-1, voff), blk(0, voff), blk(1, voff), kxspec, vxspec, mspec])
    return pl.pallas_call(
        functools.partial(_win_body, groups=groups, kv_per_step=kvs),
        grid=(bsz, A_KV_HEADS // kvs, nblk),
        in_specs=in_specs,
        out_specs=pl.BlockSpec((BLOCK, kvs * gw), lambda b, h, n: (b * nblk + n, h)),
        out_shape=jax.ShapeDtypeStruct((bsz * seq, A_HEADS * HEAD_DIM), BF16),
        compiler_params=_params(3),
        name="win_attn",
    )(sink, *([qk_a] * kvs), qk_a, qk_a, qk_a, plain, plain, plain, ropecols_c, plain_c,
      jnp.asarray(_win_masks(ctx)))


def kernel(x, c, ctx, c_ctx, w_ada, b_ada, g_mix, w_in, sink_a, g_q_a, w_q_b, g_kv_a, w_kv_b, rpb_c, w_br, w_out,
           g_ffn, w_ffn_gate, w_ffn_up, w_ffn_down, g_final):
    bsz, seq, d = x.shape
    assert (bsz, seq, d) == (BATCH, SEQ, D_MODEL) and ctx.shape == (BATCH, CTX_LEN, D_MODEL)
    assert w_in.shape[0] == DEPTH and seq % (NA_QROWS * GRID_W) == 0 and seq % BLOCK == 0
    tabs = _rope_tables(seq)
    pad = (-(bsz + 1)) % 8
    cv = jnp.concatenate([c, c_ctx[None, :], jnp.zeros((pad, d), F32)], axis=0)
    x2 = x.reshape(bsz * seq, d)
    ctx2 = ctx.reshape(bsz * ctx.shape[1], d)
    sw = _stacked_weights(w_in, w_q_b, w_kv_b, w_br, w_out, w_ffn_gate, w_ffn_up, w_ffn_down)
    for l in range(DEPTH):
        x2, ctx2 = _layer(x2, ctx2, cv, tabs, l, sw, w_ada, b_ada, g_mix[l], sink_a[l], g_q_a[l], g_kv_a[l], rpb_c[l],
                          g_ffn[l], last=(l == DEPTH - 1))
    zero = jnp.zeros((1, 1, d), F32)
    out = _norm_mod(x2, g_final, zero, zero, seq, out_dtype=F32)
    return out.reshape(bsz, seq, d)
```

```python
import functools
import math

import numpy as np
import jax
import jax.numpy as jnp
from jax import lax
from jax.experimental import pallas as pl
from jax.experimental.pallas import tpu as pltpu

D_MODEL = 4096
BATCH = 2
SEQ = 8192
DEPTH = 2
GRID_W = 64
CTX_LEN = 256
HEAD_DIM = 128
A_HEADS = 16
A_KV_HEADS = 4
WINDOW = 128
BLOCK = 128
B_HEADS = 16
Q_LORA = 1024
KV_LORA = 512
NOPE_DIM = 128
ROPE_DIM = 64
V_DIM = 128
C_HEADS = 16
NA_ROWS = 8
NA_COLS = 16
N_BRANCH = 3
ROPE_BASE = 10000.0
EPS = 1e-6
NEG_INF = -1e30

LANE = 128
MLA_QK = 2 * LANE
LOG2E = math.log2(math.e)
MLA_LOGIT_SCALE = (NOPE_DIM + ROPE_DIM) ** -0.5 * LOG2E
HEAD_LOGIT_SCALE = HEAD_DIM ** -0.5 * LOG2E
NA_QROWS = 4
VMEM_LIMIT = 56 * 1024 * 1024

BF16 = jnp.bfloat16
F32 = jnp.float32


def _params(n_axes):
    return pltpu.CompilerParams(dimension_semantics=("parallel",) * n_axes,
                                vmem_limit_bytes=VMEM_LIMIT)


def _dot(a, b):
    return jnp.dot(a, b, preferred_element_type=F32)


def _dot_nt(a, b):
    return lax.dot_general(a, b, (((1,), (1,)), ((), ())), preferred_element_type=F32)


def _tile(n, pref):
    if n <= pref:
        return n
    t = pref - pref % LANE
    while t >= LANE:
        if n % t == 0:
            return t
        t -= LANE
    return n


def _adaln_body(c_ref, w_ref, b_ref, o_ref):
    cv = c_ref[...]
    o_ref[...] = _dot(cv * jax.nn.sigmoid(cv), w_ref[...]) + b_ref[...]


def _adaln(cv, w_ada, b_ada, layer):
    rows, d = cv.shape
    n = w_ada.shape[2]
    bn = _tile(n, 512)
    return pl.pallas_call(
        _adaln_body,
        grid=(n // bn,),
        in_specs=[pl.BlockSpec((rows, d), lambda j: (0, 0)),
                  pl.BlockSpec((None, d, bn), lambda j: (layer, 0, j)),
                  pl.BlockSpec((None, 1, bn), lambda j: (layer, 0, j))],
        out_specs=pl.BlockSpec((rows, bn), lambda j: (0, j)),
        out_shape=jax.ShapeDtypeStruct((rows, n), F32),
        compiler_params=_params(1),
        name="adaln",
    )(cv, w_ada, b_ada.reshape(b_ada.shape[0], 1, n))


def _norm_mod_body(x_ref, g_ref, sh_ref, sc_ref, o_ref):
    x = x_ref[...]
    y = x * lax.rsqrt(jnp.mean(x * x, axis=-1, keepdims=True) + EPS) * g_ref[...]
    o_ref[...] = (y * (1.0 + sc_ref[...]) + sh_ref[...]).astype(o_ref.dtype)


def _norm_mod(x2, g, shift, scale, rows_per_batch, out_dtype=BF16):
    m, d = x2.shape
    ts = _tile(rows_per_batch, 512)
    tpb = rows_per_batch // ts
    per_batch = shift.shape[0] > 1
    mod_spec = pl.BlockSpec((None, 1, d), (lambda i: (i // tpb, 0, 0)) if per_batch else (lambda i: (0, 0, 0)))
    return pl.pallas_call(
        _norm_mod_body,
        grid=(m // ts,),
        in_specs=[pl.BlockSpec((ts, d), lambda i: (i, 0)),
                  pl.BlockSpec((1, d), lambda i: (0, 0)),
                  mod_spec, mod_spec],
        out_specs=pl.BlockSpec((ts, d), lambda i: (i, 0)),
        out_shape=jax.ShapeDtypeStruct((m, d), out_dtype),
        compiler_params=_params(1),
        name="norm_mod",
    )(x2, g.reshape(1, d), shift, scale)


def _a_spec(bm, k):
    return pl.BlockSpec((bm, k), lambda i, j: (i, 0))


class _LayerWeight:
    def __init__(self, arr, layer, ncols=None):
        self.arr, self.layer = arr, layer
        self.shape = (arr.shape[1], arr.shape[2] if ncols is None else ncols)

    def cols(self, ncols):
        return _LayerWeight(self.arr, self.layer, ncols)


def _w_spec(w, bn):
    return pl.BlockSpec((None, w.shape[0], bn), lambda i, j: (w.layer, 0, j))


def _mn_spec(bm, bn):
    return pl.BlockSpec((bm, bn), lambda i, j: (i, j))


def _pos_spec(bm, width, tiles_per_batch):
    return pl.BlockSpec((bm, width), lambda i, j: (i % tiles_per_batch, 0))


def _mm_call(body, grid, in_specs, out_specs, out_shape, args, name):
    args = [a.arr if isinstance(a, _LayerWeight) else a for a in args]
    return pl.pallas_call(body, grid=grid, in_specs=in_specs, out_specs=out_specs, out_shape=out_shape,
                          compiler_params=_params(2), name=name)(*args)


def _plain_body(a_ref, w_ref, o_ref):
    o_ref[...] = _dot(a_ref[...], w_ref[...]).astype(o_ref.dtype)


def _scaled_body(a_ref, w_ref, cs_ref, o_ref):
    o_ref[...] = (_dot(a_ref[...], w_ref[...]) * cs_ref[...]).astype(o_ref.dtype)


def _n_spec(bn):
    return pl.BlockSpec((1, bn), lambda i, j: (0, j))


def _mm_plain(a, w, colscale=None, bm=2048, bn=512, name="mm_plain"):
    m, k = a.shape
    n = w.shape[1]
    bm, bn = _tile(m, bm), _tile(n, bn)
    if colscale is None:
        return _mm_call(_plain_body, (m // bm, n // bn), [_a_spec(bm, k), _w_spec(w, bn)], _mn_spec(bm, bn),
                        jax.ShapeDtypeStruct((m, n), BF16), (a, w), name)
    return _mm_call(_scaled_body, (m // bm, n // bn), [_a_spec(bm, k), _w_spec(w, bn), _n_spec(bn)],
                    _mn_spec(bm, bn), jax.ShapeDtypeStruct((m, n), BF16), (a, w, colscale), name)


def _rope_chunks(acc, cos, sin):
    outs = []
    for c in range(acc.shape[1] // LANE):
        t = acc[:, c * LANE:(c + 1) * LANE]
        outs.append(t * cos + pltpu.roll(t, LANE // 2, 1) * sin)
    return outs[0] if len(outs) == 1 else jnp.concatenate(outs, axis=1)


def _rope_body(a_ref, w_ref, cs_ref, cos_ref, sin_ref, o_ref):
    acc = _dot(a_ref[...], w_ref[...]) * cs_ref[...]
    o_ref[...] = _rope_chunks(acc, cos_ref[...], sin_ref[...]).astype(o_ref.dtype)


def _mm_rope(a, w, colscale, cos2, sin2, rows_per_batch, bm=2048, bn=512):
    m, k = a.shape
    n = w.shape[1]
    bm, bn = _tile(rows_per_batch, bm), _tile(n, bn)
    tpb = rows_per_batch // bm
    return _mm_call(_rope_body, (m // bm, n // bn),
                    [_a_spec(bm, k), _w_spec(w, bn), _n_spec(bn), _pos_spec(bm, LANE, tpb), _pos_spec(bm, LANE, tpb)],
                    _mn_spec(bm, bn), jax.ShapeDtypeStruct((m, n), BF16), (a, w, colscale, cos2, sin2), "mm_rope")


def _ckv_body(a_ref, w_ref, g_ref, *rest, rope):
    if rope:
        cos_ref, sin_ref, ckv_ref, kr_ref = rest
    else:
        ckv_ref, kr_ref = rest
    acc = _dot(a_ref[...], w_ref[...])
    nl = ckv_ref.shape[1]
    t = acc[:, :nl]
    ckv_ref[...] = (t * lax.rsqrt(jnp.mean(t * t, axis=-1, keepdims=True) + EPS) * g_ref[...]).astype(ckv_ref.dtype)
    r = acc[:, nl:]
    if rope:
        r = _rope_chunks(r, cos_ref[...], sin_ref[...])
    kr_ref[...] = r.astype(kr_ref.dtype)


def _mm_ckv(a, w, g, cos2, sin2, rows_per_batch, rope, bm=1024):
    m, k = a.shape
    n = w.shape[1]
    nl = n - LANE
    bm = _tile(rows_per_batch, bm)
    tpb = rows_per_batch // bm
    in_specs = [_a_spec(bm, k), _w_spec(w, n), pl.BlockSpec((1, nl), lambda i, j: (0, 0))]
    args = [a, w, g.reshape(1, nl)]
    if rope:
        in_specs += [_pos_spec(bm, LANE, tpb), _pos_spec(bm, LANE, tpb)]
        args += [cos2, sin2]
    return _mm_call(functools.partial(_ckv_body, rope=rope), (m // bm, 1), in_specs,
                    [pl.BlockSpec((bm, nl), lambda i, j: (i, 0)), pl.BlockSpec((bm, LANE), lambda i, j: (i, 0))],
                    [jax.ShapeDtypeStruct((m, nl), BF16), jax.ShapeDtypeStruct((m, LANE), BF16)], args, "mm_ckv")


def _rms_body(a_ref, w_ref, g_ref, o_ref):
    t = _dot(a_ref[...], w_ref[...])
    o_ref[...] = (t * lax.rsqrt(jnp.mean(t * t, axis=-1, keepdims=True) + EPS) * g_ref[...]).astype(o_ref.dtype)


def _mm_rms(a, w, g, bm=1024):
    m, k = a.shape
    n = w.shape[1]
    bm = _tile(m, bm)
    return _mm_call(_rms_body, (m // bm, 1), [_a_spec(bm, k), _w_spec(w, n), pl.BlockSpec((1, n), lambda i, j: (0, 0))],
                    _mn_spec(bm, n), jax.ShapeDtypeStruct((m, n), BF16), (a, w, g.reshape(1, n)), "mm_rms")


def _sigmoid(x):
    return 0.5 * jnp.tanh(0.5 * x) + 0.5


def _sigmoid_body(a_ref, w_ref, o_ref):
    o_ref[...] = _sigmoid(_dot(a_ref[...], w_ref[...]))


def _mm_sigmoid(a, w, bm=1024, bn=1024):
    m, k = a.shape
    n = w.shape[1]
    bm, bn = _tile(m, bm), _tile(n, bn)
    return _mm_call(_sigmoid_body, (m // bm, n // bn), [_a_spec(bm, k), _w_spec(w, bn)], _mn_spec(bm, bn),
                    jax.ShapeDtypeStruct((m, n), F32), (a, w), "mm_gates")


def _qb_body(a_ref, w_ref, *rest, rope):
    if rope:
        cos_ref, sin_ref, o_ref = rest
    else:
        (o_ref,) = rest
    acc = _dot(a_ref[...], w_ref[...]) * MLA_LOGIT_SCALE
    if rope:
        parts = []
        for h in range(acc.shape[1] // MLA_QK):
            parts.append(acc[:, h * MLA_QK:h * MLA_QK + NOPE_DIM])
            parts.append(_rope_chunks(acc[:, h * MLA_QK + NOPE_DIM:(h + 1) * MLA_QK], cos_ref[...], sin_ref[...]))
        acc = jnp.concatenate(parts, axis=1)
    o_ref[...] = acc.astype(o_ref.dtype)


def _mm_qb(a, w, cos2, sin2, rows_per_batch, rope, bm=1024):
    m, k = a.shape
    n = w.shape[1]
    bm = _tile(rows_per_batch, bm)
    bn = 4 * MLA_QK if n % (4 * MLA_QK) == 0 else MLA_QK
    tpb = rows_per_batch // bm
    in_specs = [_a_spec(bm, k), _w_spec(w, bn)]
    args = [a, w]
    if rope:
        in_specs += [_pos_spec(bm, LANE, tpb), _pos_spec(bm, LANE, tpb)]
        args += [cos2, sin2]
    return _mm_call(functools.partial(_qb_body, rope=rope), (m // bm, n // bn), in_specs, _mn_spec(bm, bn),
                    jax.ShapeDtypeStruct((m, n), BF16), args, "mm_qb")


def _kexp_body(a_ref, w_ref, kr_ref, o_ref):
    acc = _dot(a_ref[...], w_ref[...]).astype(o_ref.dtype)
    kr = kr_ref[...]
    parts = []
    for h in range(acc.shape[1] // NOPE_DIM):
        parts += [acc[:, h * NOPE_DIM:(h + 1) * NOPE_DIM], kr]
    o_ref[...] = jnp.concatenate(parts, axis=1)


def _mm_kexp(ckv, w_uk, kr, bm=1536, heads_per_tile=8):
    m, k = ckv.shape
    n = w_uk.shape[1]
    heads = n // NOPE_DIM
    bm = _tile(m, bm)
    hpt = heads_per_tile if heads % heads_per_tile == 0 else 1
    return _mm_call(_kexp_body, (m // bm, heads // hpt),
                    [_a_spec(bm, k), _w_spec(w_uk, hpt * NOPE_DIM), pl.BlockSpec((bm, LANE), lambda i, j: (i, 0))],
                    _mn_spec(bm, hpt * MLA_QK), jax.ShapeDtypeStruct((m, heads * MLA_QK), BF16),
                    (ckv, w_uk, kr), "mm_kexp")


def _vexp_body(a_ref, w_ref, o_ref):
    acc = _dot(a_ref[...], w_ref[...]).astype(o_ref.dtype)
    lane = lax.broadcasted_iota(jnp.int32, (acc.shape[0], V_DIM), 1)
    ones = jnp.where(lane == 0, 1.0, 0.0).astype(o_ref.dtype)
    parts = []
    for h in range(acc.shape[1] // V_DIM):
        parts += [acc[:, h * V_DIM:(h + 1) * V_DIM], ones]
    o_ref[...] = jnp.concatenate(parts, axis=1)


def _mm_vexp(ckv, w_uv, bm=1536, heads_per_tile=8):
    m, k = ckv.shape
    heads = w_uv.shape[1] // V_DIM
    bm = _tile(m, bm)
    hpt = heads_per_tile if heads % heads_per_tile == 0 else 1
    return _mm_call(_vexp_body, (m // bm, heads // hpt), [_a_spec(bm, k), _w_spec(w_uv, hpt * V_DIM)],
                    _mn_spec(bm, 2 * hpt * V_DIM), jax.ShapeDtypeStruct((m, 2 * heads * V_DIM), BF16),
                    (ckv, w_uv), "mm_vexp")


def _merge_body(oa_ref, ob_ref, oc_ref, wa_ref, wb_ref, wc_ref, ga_ref, gb_ref, gc_ref, o_ref):
    y = ga_ref[...] * _dot(oa_ref[...], wa_ref[...])
    y = y + gb_ref[...] * _dot(ob_ref[...], wb_ref[...])
    y = y + gc_ref[...] * _dot(oc_ref[...], wc_ref[...])
    o_ref[...] = y.astype(o_ref.dtype)


def _mm_merge(o_a, o_b, o_c, w_br, layer, gates, bm=1024, bn=256):
    m, k = o_a.shape
    n = w_br.shape[3]
    bm, bn = _tile(m, bm), _tile(n, bn)
    nb = n // bn
    wspec = [pl.BlockSpec((None, None, k, bn), (lambda i, j, r=r: (layer, r, 0, j))) for r in range(N_BRANCH)]
    gspec = [pl.BlockSpec((bm, bn), (lambda i, j, r=r: (i, r * nb + j))) for r in range(N_BRANCH)]
    return _mm_call(_merge_body, (m // bm, nb), [_a_spec(bm, k)] * 3 + wspec + gspec, _mn_spec(bm, bn),
                    jax.ShapeDtypeStruct((m, n), BF16), (o_a, o_b, o_c, w_br, w_br, w_br, gates, gates, gates),
                    "mm_merge")


def _resid_body(a_ref, w_ref, x_ref, gt_ref, o_ref):
    o_ref[...] = x_ref[...] + gt_ref[...] * _dot(a_ref[...], w_ref[...])


def _mm_resid(a, w, x2, gate, rows_per_batch, bm=1024, bn=512, name="mm_resid"):
    m, k = a.shape
    n = w.shape[1]
    bm, bn = _tile(rows_per_batch, bm), _tile(n, bn)
    tpb = rows_per_batch // bm
    per_batch = gate.shape[0] > 1
    gspec = pl.BlockSpec((None, 1, bn), (lambda i, j: (i // tpb, 0, j)) if per_batch else (lambda i, j: (0, 0, j)))
    return _mm_call(_resid_body, (m // bm, n // bn), [_a_spec(bm, k), _w_spec(w, bn), _mn_spec(bm, bn), gspec],
                    _mn_spec(bm, bn), jax.ShapeDtypeStruct((m, n), F32), (a, w, x2, gate), name)


def _ffn_up_body(a_ref, wg_ref, wu_ref, o_ref):
    a = a_ref[...]
    g = _dot(a, wg_ref[...])
    o_ref[...] = (g * _sigmoid(g) * _dot(a, wu_ref[...])).astype(o_ref.dtype)


def _mm_ffn_up(a, wg, wu, bm=2048, bn=256):
    m, k = a.shape
    n = wg.shape[1]
    bm, bn = _tile(m, bm), _tile(n, bn)
    return _mm_call(_ffn_up_body, (m // bm, n // bn), [_a_spec(bm, k), _w_spec(wg, bn), _w_spec(wu, bn)],
                    _mn_spec(bm, bn), jax.ShapeDtypeStruct((m, n), BF16), (a, wg, wu), "mm_ffn_up")


def _with_ones_col(v):
    lane = lax.broadcasted_iota(jnp.int32, v.shape, 1)
    return jnp.concatenate([v, jnp.where(lane == 0, 1.0, 0.0).astype(v.dtype)], axis=1)


def _win_body(sink_ref, *refs, groups, kv_per_step):
    q_refs = refs[:kv_per_step]
    kp_ref, kc_ref, kn_ref, vp_ref, vc_ref, vn_ref, kx_ref, vx_ref, mask_ref, o_ref = refs[kv_per_step:]
    kv0 = pl.program_id(1) * kv_per_step
    mask = mask_ref[...]
    chains = [(j, g) for j in range(kv_per_step) for g in range(groups)]
    k_all, v_all = [], []
    for j in range(kv_per_step):
        c = slice(j * HEAD_DIM, (j + 1) * HEAD_DIM)
        k_all.append(jnp.concatenate([kp_ref[:, c], kc_ref[:, c], kn_ref[:, c], kx_ref[:, c]], axis=0))
        v_all.append(_with_ones_col(jnp.concatenate([vp_ref[:, c], vc_ref[:, c], vn_ref[:, c], vx_ref[:, c]], axis=0)))
    s = [_dot_nt(q_refs[j][:, g * HEAD_DIM:(g + 1) * HEAD_DIM], k_all[j]) + mask for j, g in chains]
    sink = [sink_ref[(kv0 + j) * groups + g] * LOG2E for j, g in chains]
    m = [jnp.maximum(jnp.max(si, axis=1, keepdims=True), sk) for si, sk in zip(s, sink)]
    p = [jnp.exp2(si - mi).astype(BF16) for si, mi in zip(s, m)]
    for i, (j, g) in enumerate(chains):
        acc = _dot(p[i], v_all[j])
        l = acc[:, HEAD_DIM:HEAD_DIM + 1] + jnp.exp2(sink[i] - m[i])
        c0 = (j * groups + g) * HEAD_DIM
        o_ref[:, c0:c0 + HEAD_DIM] = (acc[:, :HEAD_DIM] / l).astype(o_ref.dtype)


def _win_masks(ctx):
    qi = np.arange(BLOCK)[:, None]
    kj = np.arange(3 * BLOCK)[None, :]
    band = np.abs(kj - qi - BLOCK) <= WINDOW
    out = []
    for lo, hi in ((BLOCK, 3 * BLOCK), (0, 3 * BLOCK), (0, 2 * BLOCK)):
        ok = band & (kj >= lo) & (kj < hi)
        out.append(np.concatenate([np.where(ok, 0.0, NEG_INF), np.zeros((BLOCK, ctx))], axis=1))
    return np.stack(out).astype(np.float32)


def _dense_body(*refs, tk, scale, has_sink, ones_col):
    if has_sink:
        sink_ref, q_ref, k_ref, v_ref, o_ref = refs
    else:
        q_ref, k_ref, v_ref, o_ref = refs
    q = q_ref[...]
    lk = k_ref.shape[0]
    dv = o_ref.shape[1]
    exp = jnp.exp2 if scale is None else jnp.exp
    m = l = acc = None
    nchunks = lk // tk

    def logits(c):
        s = _dot_nt(q, k_ref[c * tk:(c + 1) * tk, :])
        return s if scale is None else s * scale

    for c in range(nchunks):
        s = logits(c)
        mc = jnp.max(s, axis=1, keepdims=True)
        if c == 0:
            m_new = mc
            if has_sink:
                sink = sink_ref[pl.program_id(1)]
                m_new = jnp.maximum(m_new, sink)
        else:
            m_new = jnp.maximum(m, mc)
        p = exp(s - m_new)
        pv = _dot(p.astype(BF16), v_ref[c * tk:(c + 1) * tk, :])
        if not ones_col:
            ps = jnp.sum(p, axis=1, keepdims=True)
            if c == 0:
                l = ps + exp(sink - m_new) if has_sink else ps
            else:
                l = exp(m - m_new) * l + ps
        acc = pv if c == 0 else exp(m - m_new) * acc + pv
        m = m_new
    if ones_col:
        l = acc[:, dv:dv + 1]
        acc = acc[:, :dv]
    o_ref[...] = (acc / l).astype(o_ref.dtype)


def _dense_attention(q3, k3, v3, *, heads, groups, dk, dv, q_off, k_off, v_off, lk, tq, tk, scale, sink=None,
                     ones_col=False, name="dense_attn"):
    assert not (ones_col and sink is not None)
    bsz, lq, _ = q3.shape
    tq = _tile(lq, tq)
    dvw = 2 * dv if ones_col else dv
    in_specs = [pl.BlockSpec((None, tq, dk), lambda b, h, i: (b, i, q_off + h)),
                pl.BlockSpec((None, lk, dk), lambda b, h, i: (b, 0, k_off + h // groups)),
                pl.BlockSpec((None, lk, dvw), lambda b, h, i: (b, 0, v_off + h // groups))]
    args = [q3, k3, v3]
    if sink is not None:
        in_specs = [pl.BlockSpec(memory_space=pltpu.SMEM)] + in_specs
        args = [sink] + args
    return pl.pallas_call(
        functools.partial(_dense_body, tk=tk, scale=scale, has_sink=sink is not None, ones_col=ones_col),
        grid=(bsz, heads, lq // tq),
        in_specs=in_specs,
        out_specs=pl.BlockSpec((None, tq, dv), lambda b, h, i: (b, i, h)),
        out_shape=jax.ShapeDtypeStruct((bsz, lq, heads * dv), BF16),
        compiler_params=_params(3),
        name=name,
    )(*args)


def _na_body(q_ref, kp_ref, kc_ref, kn_ref, vp_ref, vc_ref, vn_ref, kx_ref, vx_ref, bias_ref, o_ref):
    heads = range(bias_ref.shape[0])
    cols = [slice(h * HEAD_DIM, (h + 1) * HEAD_DIM) for h in heads]
    s = [_dot_nt(q_ref[:, c], jnp.concatenate([kp_ref[:, c], kc_ref[:, c], kn_ref[:, c], kx_ref[:, c]], axis=0))
         + bias_ref[h] for h, c in zip(heads, cols)]
    p = [jnp.exp2(sh - jnp.max(sh, axis=1, keepdims=True)).astype(BF16) for sh in s]
    for h, c in zip(heads, cols):
        v_all = _with_ones_col(jnp.concatenate([vp_ref[:, c], vc_ref[:, c], vn_ref[:, c], vx_ref[:, c]], axis=0))
        acc = _dot(p[h], v_all)
        o_ref[:, c] = (acc[:, :HEAD_DIM] / acc[:, HEAD_DIM:HEAD_DIM + 1]).astype(o_ref.dtype)


def _na_valid(rows):
    r = NA_QROWS
    nblk = rows // r
    assert rows % r == 0 and nblk >= 3 and rows >= NA_ROWS and NA_ROWS >= 2 * r
    kwin = min(NA_ROWS, rows)
    valid = []
    for i in (0, 1, nblk - 1):
        qrow = (r * i + np.arange(r))[:, None, None, None]
        qcol = np.arange(GRID_W)[None, :, None, None]
        krow = (r * (i - 1) + np.arange(3 * r))[None, None, :, None]
        kcol = np.arange(GRID_W)[None, None, None, :]
        r0 = np.clip(qrow - kwin // 2, 0, rows - kwin)
        c0 = np.clip(qcol - NA_COLS // 2, 0, GRID_W - NA_COLS)
        ok = (krow >= 0) & (krow < rows) & (krow >= r0) & (krow < r0 + kwin) & (kcol >= c0) & (kcol < c0 + NA_COLS)
        valid.append(ok.reshape(r * GRID_W, 3 * r * GRID_W))
    return np.stack(valid)


def _na_bias_tables(rpb, rows, ctx):
    r, w = NA_QROWS, GRID_W
    heads, nro, nco = rpb.shape
    circ = jnp.concatenate([rpb[..., NA_COLS - 1:], jnp.zeros((heads, nro, 2 * w - nco), rpb.dtype),
                            rpb[..., :NA_COLS - 1]], axis=-1)
    toep = jnp.broadcast_to(circ[:, :, None, :], (heads, nro, w, 2 * w)).reshape(heads, nro, 2 * w * w)
    toep = toep[:, :, :w * (2 * w - 1)].reshape(heads, nro, w, 2 * w - 1)[..., :w]
    strips = []
    for qr in range(r):
        lo = NA_ROWS - 1 - r - qr
        strips.append(toep[:, lo:lo + 3 * r].transpose(0, 2, 1, 3).reshape(heads, w, 3 * r * w))
    tile = jnp.concatenate(strips, axis=1) * LOG2E
    local = jnp.where(jnp.asarray(_na_valid(rows))[None], tile[:, None], NEG_INF)
    return jnp.concatenate([local, jnp.zeros(local.shape[:3] + (ctx,), local.dtype)], axis=-1)


def _neighborhood_attention(plain, plain_c, rpb, bsz, seq, q_blk, k_blk, v_blk, heads_per_step):
    rows = seq // GRID_W
    tq = NA_QROWS * GRID_W
    nblk = rows // NA_QROWS
    ctx = plain_c.shape[0] // bsz
    hb = heads_per_step
    hw = hb * HEAD_DIM
    bias = _na_bias_tables(rpb, rows, ctx)

    def blk(off, d):
        def imap(b, h, i):
            return (b * nblk + jnp.clip(i + d, 0, nblk - 1), off + h)
        return pl.BlockSpec((tq, hw), imap)

    def bias_map(b, h, i):
        return (h, jnp.where(i == 0, 0, jnp.where(i == nblk - 1, 2, 1)), 0, 0)

    in_specs = [blk(q_blk, 0),
                blk(k_blk, -1), blk(k_blk, 0), blk(k_blk, 1),
                blk(v_blk, -1), blk(v_blk, 0), blk(v_blk, 1),
                pl.BlockSpec((ctx, hw), lambda b, h, i: (b, k_blk + h)),
                pl.BlockSpec((ctx, hw), lambda b, h, i: (b, v_blk + h)),
                pl.BlockSpec((hb, None, tq, 3 * tq + ctx), bias_map)]
    return pl.pallas_call(
        _na_body,
        grid=(bsz, C_HEADS // hb, nblk),
        in_specs=in_specs,
        out_specs=pl.BlockSpec((tq, hw), lambda b, h, i: (b * nblk + i, h)),
        out_shape=jax.ShapeDtypeStruct((bsz * seq, C_HEADS * HEAD_DIM), BF16),
        compiler_params=_params(3),
        name="na_attn",
    )(plain, plain, plain, plain, plain, plain, plain, plain_c, plain_c, bias)


def _rope_tables(seq):
    t = jnp.arange(seq, dtype=jnp.int32)
    row = (t // GRID_W).astype(F32)
    col = (t % GRID_W).astype(F32)

    def cs(rot_dim):
        n_freq = rot_dim // 4
        inv = jnp.power(ROPE_BASE, -jnp.arange(n_freq, dtype=F32) / n_freq)
        ang = jnp.concatenate([row[:, None] * inv, col[:, None] * inv], axis=-1)
        return jnp.cos(ang), jnp.sin(ang)

    ca, sa = cs(HEAD_DIM)
    cb, sb = cs(ROPE_DIM)
    zb = jnp.zeros_like(cb)
    return (jnp.concatenate([ca, ca], axis=1), jnp.concatenate([-sa, sa], axis=1),
            jnp.concatenate([cb, zb, cb, zb], axis=1), jnp.concatenate([-sb, zb, sb, zb], axis=1))


def _spread_rope_cols(w):
    h = ROPE_DIM // 2
    z = jnp.zeros(w.shape[:-1] + (LANE // 2 - h,), w.dtype)
    return jnp.concatenate([w[..., :h], z, w[..., h:], z], axis=-1)


def _stacked_weights(w_in, w_q_b, w_kv_b, w_br, w_out, w_ffn_gate, w_ffn_up, w_ffn_down):
    nl = w_in.shape[0]
    kvh = A_KV_HEADS * HEAD_DIM
    ch = C_HEADS * HEAD_DIM
    ah = A_HEADS * HEAD_DIM
    sizes = [kvh, kvh, KV_LORA, ROPE_DIM, ch, ch, ah, Q_LORA, ch, N_BRANCH * D_MODEL]
    offs = np.concatenate([[0], np.cumsum(sizes)])
    a_k, a_v, b_ckv, b_kr, c_k, c_v, a_q, b_qa, c_q, gate = [w_in[..., offs[i]:offs[i + 1]]
                                                             for i in range(len(sizes))]
    return {
        "rope": jnp.concatenate([a_k, a_q], axis=-1).astype(BF16),
        "plain": jnp.concatenate([c_k, c_v, c_q, a_v], axis=-1).astype(BF16),
        "ckv": jnp.concatenate([b_ckv, _spread_rope_cols(b_kr)], axis=-1).astype(BF16),
        "bqa": b_qa.astype(BF16),
        "gate": gate.astype(BF16),
        "qb": jnp.concatenate([w_q_b[..., :NOPE_DIM], _spread_rope_cols(w_q_b[..., NOPE_DIM:])], axis=-1)
        .reshape(nl, Q_LORA, B_HEADS * MLA_QK).astype(BF16),
        "uk": w_kv_b[..., :NOPE_DIM].reshape(nl, KV_LORA, B_HEADS * NOPE_DIM).astype(BF16),
        "uv": w_kv_b[..., NOPE_DIM:].reshape(nl, KV_LORA, B_HEADS * V_DIM).astype(BF16),
        "br": w_br.astype(BF16),
        "out": w_out.astype(BF16),
        "ffn_gate": w_ffn_gate.astype(BF16),
        "ffn_up": w_ffn_up.astype(BF16),
        "ffn_down": w_ffn_down.astype(BF16),
    }


def _layer(x2, ctx2, cv, tabs, layer, sw, w_ada, b_ada, g_mix, sink_a, g_q_a, g_kv_a, rpb_c, g_ffn, last):
    bsz, seq, ctx, d = BATCH, SEQ, CTX_LEN, D_MODEL
    cos_a, sin_a, cos_b, sin_b = tabs
    kvb = A_KV_HEADS * HEAD_DIM // LANE
    chb = C_HEADS * HEAD_DIM // LANE
    na_hb = math.gcd(chb, 8)

    mod = _adaln(cv, w_ada, b_ada, layer)
    sh_m, sc_m, gt_m, sh_f, sc_f, gt_f = [mod[:bsz, i * d:(i + 1) * d].reshape(bsz, 1, d) for i in range(6)]
    csh_m, csc_m, cgt_m, csh_f, csc_f, cgt_f = [mod[bsz:bsz + 1, i * d:(i + 1) * d].reshape(1, 1, d) for i in range(6)]
    w = {name: _LayerWeight(arr, layer) for name, arr in sw.items() if name != "br"}

    cm = bsz * ctx
    hc = _norm_mod(ctx2, g_mix, csh_m, csc_m, cm)
    ropecols_c = _mm_plain(hc, w["rope"] if not last else w["rope"].cols(kvb * LANE), bm=cm, name="mm_plain_c")
    plain_c = _mm_plain(hc, w["plain"], bm=cm, name="mm_plain_c")
    ckv_c, kr_c = _mm_ckv(hc, w["ckv"], g_kv_a, None, None, cm, rope=False, bm=cm)

    hx = _norm_mod(x2, g_mix, sh_m, sc_m, seq)
    qscale = jnp.full((1, A_HEADS * HEAD_DIM), HEAD_LOGIT_SCALE, F32)
    cs_rope = jnp.concatenate([jnp.ones((1, kvb * LANE), F32), qscale], axis=1)
    cs_plain = jnp.concatenate([jnp.ones((1, 2 * chb * LANE), F32), jnp.full((1, chb * LANE), HEAD_LOGIT_SCALE, F32),
                                jnp.ones((1, kvb * LANE), F32)], axis=1)
    qk_a = _mm_rope(hx, w["rope"], cs_rope, cos_a, sin_a, seq)
    plain = _mm_plain(hx, w["plain"], cs_plain)
    ckv, kr = _mm_ckv(hx, w["ckv"], g_kv_a, cos_b, sin_b, seq, rope=True)
    bqa = _mm_rms(hx, w["bqa"], g_q_a)
    gates = _mm_sigmoid(hx, w["gate"])
    qb = _mm_qb(bqa, w["qb"], cos_b, sin_b, seq, rope=True)

    lk = ctx + seq
    ckv_all = jnp.concatenate([ckv_c.reshape(bsz, ctx, -1), ckv.reshape(bsz, seq, -1)], axis=1).reshape(bsz * lk, -1)
    kr_all = jnp.concatenate([kr_c.reshape(bsz, ctx, -1), kr.reshape(bsz, seq, -1)], axis=1).reshape(bsz * lk, -1)
    k_exp = _mm_kexp(ckv_all, w["uk"], kr_all).reshape(bsz, lk, -1)
    v_exp = _mm_vexp(ckv_all, w["uv"]).reshape(bsz, lk, -1)

    o_a = _window_attention(sink_a, qk_a, plain, ropecols_c, plain_c, bsz, seq, kvb, 3 * chb)
    o_b = _dense_attention(qb.reshape(bsz, seq, -1), k_exp, v_exp, heads=B_HEADS, groups=1, dk=MLA_QK, dv=V_DIM,
                           q_off=0, k_off=0, v_off=0, lk=lk, tq=1024, tk=_mla_chunk(lk), scale=None, ones_col=True,
                           name="mla_attn").reshape(bsz * seq, -1)
    o_c = _neighborhood_attention(plain, plain_c, rpb_c, bsz, seq, q_blk=2 * chb // na_hb, k_blk=0,
                                  v_blk=chb // na_hb, heads_per_step=na_hb)

    y = _mm_merge(o_a, o_b, o_c, sw["br"], layer, gates)
    x2 = _mm_resid(y, w["out"], x2, gt_m, seq)
    h2 = _norm_mod(x2, g_ffn, sh_f, sc_f, seq)
    act = _mm_ffn_up(h2, w["ffn_gate"], w["ffn_up"])
    x2 = _mm_resid(act, w["ffn_down"], x2, gt_f, seq, bm=512, bn=512, name="mm_ffn_down")

    if not last:
        gates_c = _mm_sigmoid(hc, w["gate"], bm=cm)
        bqa_c = _mm_rms(hc, w["bqa"], g_q_a, bm=cm)
        qb_c = _mm_qb(bqa_c, w["qb"], None, None, cm, rope=False, bm=cm)
        rc3 = ropecols_c.reshape(bsz, ctx, -1)
        pc3 = plain_c.reshape(bsz, ctx, -1)
        oc_a = _dense_attention(rc3, rc3, pc3, heads=A_HEADS, groups=A_HEADS // A_KV_HEADS, dk=HEAD_DIM, dv=HEAD_DIM,
                                q_off=kvb, k_off=0, v_off=3 * chb, lk=ctx, tq=ctx, tk=ctx, scale=HEAD_DIM ** -0.5,
                                sink=sink_a, name="ctx_attn_a").reshape(bsz * ctx, -1)
        oc_b = _dense_attention(qb_c.reshape(bsz, ctx, -1), k_exp, v_exp, heads=B_HEADS, groups=1, dk=MLA_QK,
                                dv=V_DIM, q_off=0, k_off=0, v_off=0, lk=ctx, tq=ctx, tk=ctx, scale=None, ones_col=True,
                                name="ctx_attn_b").reshape(bsz * ctx, -1)
        oc_c = _dense_attention(pc3, pc3, pc3, heads=C_HEADS, groups=1, dk=HEAD_DIM, dv=HEAD_DIM,
                                q_off=2 * chb, k_off=0, v_off=chb, lk=ctx, tq=ctx, tk=ctx,
                                scale=HEAD_DIM ** -0.5, name="ctx_attn_c").reshape(bsz * ctx, -1)
        yc = _mm_merge(oc_a, oc_b, oc_c, sw["br"], layer, gates_c, bm=cm, bn=512)
        ctx2 = _mm_resid(yc, w["out"], ctx2, cgt_m, cm, bm=cm, name="mm_resid_c")
        hc2 = _norm_mod(ctx2, g_ffn, csh_f, csc_f, cm)
        act_c = _mm_ffn_up(hc2, w["ffn_gate"], w["ffn_up"], bm=cm)
        ctx2 = _mm_resid(act_c, w["ffn_down"], ctx2, cgt_f, cm, bm=cm, bn=256, name="mm_ffn_down_c")
    return x2, ctx2


def _mla_chunk(lk):
    for parts in range(40, 0, -1):
        if lk % parts == 0 and (lk // parts) % (2 * LANE) == 0:
            return lk // parts
    return lk


def _window_attention(sink, qk_a, plain, ropecols_c, plain_c, bsz, seq, kvb, v_lane_blk):
    nblk = seq // BLOCK
    groups = A_HEADS // A_KV_HEADS
    gw = groups * HEAD_DIM
    ctx = plain_c.shape[0] // bsz
    kvs = math.gcd(A_KV_HEADS, 4)
    kw = kvs * HEAD_DIM

    voff = v_lane_blk // kvs

    def blk(d, off):
        def imap(b, h, n):
            return (b * nblk + jnp.clip(n + d, 0, nblk - 1), off + h)
        return pl.BlockSpec((BLOCK, kw), imap)

    def qspec(j):
        return pl.BlockSpec((BLOCK, gw), lambda b, h, n: (b * nblk + n, kvb // groups + kvs * h + j))

    kxspec = pl.BlockSpec((ctx, kw), lambda b, h, n: (b, h))
    vxspec = pl.BlockSpec((ctx, kw), lambda b, h, n: (b, voff + h))
    mspec = pl.BlockSpec((None, BLOCK, 3 * BLOCK + ctx),
                         lambda b, h, n: (jnp.where(n == 0, 0, jnp.where(n == nblk - 1, 2, 1)), 0, 0))
    in_specs = ([pl.BlockSpec(memory_space=pltpu.SMEM)] + [qspec(j) for j in range(kvs)]
                + [blk(-1, 0), blk(0, 0), blk(1, 0), blk(-1, voff), blk(0, voff), blk(1, voff), kxspec, vxspec, mspec])
    return pl.pallas_call(
        functools.partial(_win_body, groups=groups, kv_per_step=kvs),
        grid=(bsz, A_KV_HEADS // kvs, nblk),
        in_specs=in_specs,
        out_specs=pl.BlockSpec((BLOCK, kvs * gw), lambda b, h, n: (b * nblk + n, h)),
        out_shape=jax.ShapeDtypeStruct((bsz * seq, A_HEADS * HEAD_DIM), BF16),
        compiler_params=_params(3),
        name="win_attn",
    )(sink, *([qk_a] * kvs), qk_a, qk_a, qk_a, plain, plain, plain, ropecols_c, plain_c,
      jnp.asarray(_win_masks(ctx)))


def kernel(x, c, ctx, c_ctx, w_ada, b_ada, g_mix, w_in, sink_a, g_q_a, w_q_b, g_kv_a, w_kv_b, rpb_c, w_br, w_out,
           g_ffn, w_ffn_gate, w_ffn_up, w_ffn_down, g_final):
    bsz, seq, d = x.shape
    tabs = _rope_tables(seq)
    pad = (-(bsz + 1)) % 8
    cv = jnp.concatenate([c, c_ctx[None, :], jnp.zeros((pad, d), F32)], axis=0)
    x2 = x.reshape(bsz * seq, d)
    ctx2 = ctx.reshape(bsz * ctx.shape[1], d)
    sw = _stacked_weights(w_in, w_q_b, w_kv_b, w_br, w_out, w_ffn_gate, w_ffn_up, w_ffn_down)
    for l in range(DEPTH):
        x2, ctx2 = _layer(x2, ctx2, cv, tabs, l, sw, w_ada, b_ada, g_mix[l], sink_a[l], g_q_a[l], g_kv_a[l], rpb_c[l],
                          g_ffn[l], last=(l == DEPTH - 1))
    zero = jnp.zeros((1, 1, d), F32)
    out = _norm_mod(x2, g_final, zero, zero, seq, out_dtype=F32)
    return out.reshape(bsz, seq, d)
```

```python
import functools
import math

import numpy as np
import jax
import jax.numpy as jnp
from jax import lax
from jax.experimental import pallas as pl
from jax.experimental.pallas import tpu as pltpu

D_MODEL = 4096
BATCH = 2
SEQ = 8192
DEPTH = 2
GRID_W = 64
CTX_LEN = 256
HEAD_DIM = 128
A_HEADS = 16
A_KV_HEADS = 4
WINDOW = 128
BLOCK = 128
B_HEADS = 16
Q_LORA = 1024
KV_LORA = 512
NOPE_DIM = 128
ROPE_DIM = 64
V_DIM = 128
C_HEADS = 16
NA_ROWS = 8
NA_COLS = 16
N_BRANCH = 3
ROPE_BASE = 10000.0
EPS = 1e-6
NEG_INF = -1e30

LANE = 128
MLA_QK = 2 * LANE
LOG2E = math.log2(math.e)
MLA_LOGIT_SCALE = (NOPE_DIM + ROPE_DIM) ** -0.5 * LOG2E
HEAD_LOGIT_SCALE = HEAD_DIM ** -0.5 * LOG2E
NA_QROWS = 4
VMEM_LIMIT = 56 * 1024 * 1024

BF16 = jnp.bfloat16
F32 = jnp.float32


def _params(n_axes):
    return pltpu.CompilerParams(dimension_semantics=("parallel",) * n_axes,
                                vmem_limit_bytes=VMEM_LIMIT)


def _dot(a, b):
    return jnp.dot(a, b, preferred_element_type=F32)


def _dot_nt(a, b):
    return lax.dot_general(a, b, (((1,), (1,)), ((), ())), preferred_element_type=F32)


def _tile(n, pref):
    if n <= pref:
        return n
    t = pref - pref % LANE
    while t >= LANE:
        if n % t == 0:
            return t
        t -= LANE
    return n


def _adaln_body(c_ref, w_ref, b_ref, o_ref):
    cv = c_ref[...]
    o_ref[...] = _dot(cv * jax.nn.sigmoid(cv), w_ref[...]) + b_ref[...]


def _adaln(cv, w_ada, b_ada, layer):
    rows, d = cv.shape
    n = w_ada.shape[2]
    bn = _tile(n, 512)
    return pl.pallas_call(
        _adaln_body,
        grid=(n // bn,),
        in_specs=[pl.BlockSpec((rows, d), lambda j: (0, 0)),
                  pl.BlockSpec((None, d, bn), lambda j: (layer, 0, j)),
                  pl.BlockSpec((None, 1, bn), lambda j: (layer, 0, j))],
        out_specs=pl.BlockSpec((rows, bn), lambda j: (0, j)),
        out_shape=jax.ShapeDtypeStruct((rows, n), F32),
        compiler_params=_params(1),
        name="adaln",
    )(cv, w_ada, b_ada.reshape(b_ada.shape[0], 1, n))


def _norm_mod_body(x_ref, g_ref, sh_ref, sc_ref, o_ref):
    x = x_ref[...]
    y = x * lax.rsqrt(jnp.mean(x * x, axis=-1, keepdims=True) + EPS) * g_ref[...]
    o_ref[...] = (y * (1.0 + sc_ref[...]) + sh_ref[...]).astype(o_ref.dtype)


def _norm_mod(x2, g, shift, scale, rows_per_batch, out_dtype=BF16):
    m, d = x2.shape
    ts = _tile(rows_per_batch, 512)
    tpb = rows_per_batch // ts
    per_batch = shift.shape[0] > 1
    mod_spec = pl.BlockSpec((None, 1, d), (lambda i: (i // tpb, 0, 0)) if per_batch else (lambda i: (0, 0, 0)))
    return pl.pallas_call(
        _norm_mod_body,
        grid=(m // ts,),
        in_specs=[pl.BlockSpec((ts, d), lambda i: (i, 0)),
                  pl.BlockSpec((1, d), lambda i: (0, 0)),
                  mod_spec, mod_spec],
        out_specs=pl.BlockSpec((ts, d), lambda i: (i, 0)),
        out_shape=jax.ShapeDtypeStruct((m, d), out_dtype),
        compiler_params=_params(1),
        name="norm_mod",
    )(x2, g.reshape(1, d), shift, scale)


def _a_spec(bm, k):
    return pl.BlockSpec((bm, k), lambda i, j: (i, 0))


class _LayerWeight:
    def __init__(self, arr, layer, ncols=None):
        self.arr, self.layer = arr, layer
        self.shape = (arr.shape[1], arr.shape[2] if ncols is None else ncols)

    def cols(self, ncols):
        return _LayerWeight(self.arr, self.layer, ncols)


def _w_spec(w, bn):
    return pl.BlockSpec((None, w.shape[0], bn), lambda i, j: (w.layer, 0, j))


def _mn_spec(bm, bn):
    return pl.BlockSpec((bm, bn), lambda i, j: (i, j))


def _pos_spec(bm, width, tiles_per_batch):
    return pl.BlockSpec((bm, width), lambda i, j: (i % tiles_per_batch, 0))


def _mm_call(body, grid, in_specs, out_specs, out_shape, args, name):
    args = [a.arr if isinstance(a, _LayerWeight) else a for a in args]
    return pl.pallas_call(body, grid=grid, in_specs=in_specs, out_specs=out_specs, out_shape=out_shape,
                          compiler_params=_params(2), name=name)(*args)


def _plain_body(a_ref, w_ref, o_ref):
    o_ref[...] = _dot(a_ref[...], w_ref[...]).astype(o_ref.dtype)


def _scaled_body(a_ref, w_ref, cs_ref, o_ref):
    o_ref[...] = (_dot(a_ref[...], w_ref[...]) * cs_ref[...]).astype(o_ref.dtype)


def _n_spec(bn):
    return pl.BlockSpec((1, bn), lambda i, j: (0, j))


def _mm_plain(a, w, colscale=None, bm=2048, bn=512, name="mm_plain"):
    m, k = a.shape
    n = w.shape[1]
    bm, bn = _tile(m, bm), _tile(n, bn)
    if colscale is None:
        return _mm_call(_plain_body, (m // bm, n // bn), [_a_spec(bm, k), _w_spec(w, bn)], _mn_spec(bm, bn),
                        jax.ShapeDtypeStruct((m, n), BF16), (a, w), name)
    return _mm_call(_scaled_body, (m // bm, n // bn), [_a_spec(bm, k), _w_spec(w, bn), _n_spec(bn)],
                    _mn_spec(bm, bn), jax.ShapeDtypeStruct((m, n), BF16), (a, w, colscale), name)


def _rope_chunks(acc, cos, sin):
    outs = []
    for c in range(acc.shape[1] // LANE):
        t = acc[:, c * LANE:(c + 1) * LANE]
        outs.append(t * cos + pltpu.roll(t, LANE // 2, 1) * sin)
    return outs[0] if len(outs) == 1 else jnp.concatenate(outs, axis=1)


def _rope_body(a_ref, w_ref, cs_ref, cos_ref, sin_ref, o_ref):
    acc = _dot(a_ref[...], w_ref[...]) * cs_ref[...]
    o_ref[...] = _rope_chunks(acc, cos_ref[...], sin_ref[...]).astype(o_ref.dtype)


def _mm_rope(a, w, colscale, cos2, sin2, rows_per_batch, bm=2048, bn=512):
    m, k = a.shape
    n = w.shape[1]
    bm, bn = _tile(rows_per_batch, bm), _tile(n, bn)
    tpb = rows_per_batch // bm
    return _mm_call(_rope_body, (m // bm, n // bn),
                    [_a_spec(bm, k), _w_spec(w, bn), _n_spec(bn), _pos_spec(bm, LANE, tpb), _pos_spec(bm, LANE, tpb)],
                    _mn_spec(bm, bn), jax.ShapeDtypeStruct((m, n), BF16), (a, w, colscale, cos2, sin2), "mm_rope")


def _ckv_body(a_ref, w_ref, g_ref, *rest, rope):
    if rope:
        cos_ref, sin_ref, ckv_ref, kr_ref = rest
    else:
        ckv_ref, kr_ref = rest
    acc = _dot(a_ref[...], w_ref[...])
    nl = ckv_ref.shape[1]
    t = acc[:, :nl]
    ckv_ref[...] = (t * lax.rsqrt(jnp.mean(t * t, axis=-1, keepdims=True) + EPS) * g_ref[...]).astype(ckv_ref.dtype)
    r = acc[:, nl:]
    if rope:
        r = _rope_chunks(r, cos_ref[...], sin_ref[...])
    kr_ref[...] = r.astype(kr_ref.dtype)


def _mm_ckv(a, w, g, cos2, sin2, rows_per_batch, rope, bm=1024):
    m, k = a.shape
    n = w.shape[1]
    nl = n - LANE
    bm = _tile(rows_per_batch, bm)
    tpb = rows_per_batch // bm
    in_specs = [_a_spec(bm, k), _w_spec(w, n), pl.BlockSpec((1, nl), lambda i, j: (0, 0))]
    args = [a, w, g.reshape(1, nl)]
    if rope:
        in_specs += [_pos_spec(bm, LANE, tpb), _pos_spec(bm, LANE, tpb)]
        args += [cos2, sin2]
    return _mm_call(functools.partial(_ckv_body, rope=rope), (m // bm, 1), in_specs,
                    [pl.BlockSpec((bm, nl), lambda i, j: (i, 0)), pl.BlockSpec((bm, LANE), lambda i, j: (i, 0))],
                    [jax.ShapeDtypeStruct((m, nl), BF16), jax.ShapeDtypeStruct((m, LANE), BF16)], args, "mm_ckv")


def _rms_body(a_ref, w_ref, g_ref, o_ref):
    t = _dot(a_ref[...], w_ref[...])
    o_ref[...] = (t * lax.rsqrt(jnp.mean(t * t, axis=-1, keepdims=True) + EPS) * g_ref[...]).astype(o_ref.dtype)


def _mm_rms(a, w, g, bm=1024):
    m, k = a.shape
    n = w.shape[1]
    bm = _tile(m, bm)
    return _mm_call(_rms_body, (m // bm, 1), [_a_spec(bm, k), _w_spec(w, n), pl.BlockSpec((1, n), lambda i, j: (0, 0))],
                    _mn_spec(bm, n), jax.ShapeDtypeStruct((m, n), BF16), (a, w, g.reshape(1, n)), "mm_rms")


def _sigmoid(x):
    return 0.5 * jnp.tanh(0.5 * x) + 0.5


def _sigmoid_body(a_ref, w_ref, o_ref):
    o_ref[...] = _sigmoid(_dot(a_ref[...], w_ref[...]))


def _mm_sigmoid(a, w, bm=1024, bn=1024):
    m, k = a.shape
    n = w.shape[1]
    bm, bn = _tile(m, bm), _tile(n, bn)
    return _mm_call(_sigmoid_body, (m // bm, n // bn), [_a_spec(bm, k), _w_spec(w, bn)], _mn_spec(bm, bn),
                    jax.ShapeDtypeStruct((m, n), F32), (a, w), "mm_gates")


def _qb_body(a_ref, w_ref, *rest, rope):
    if rope:
        cos_ref, sin_ref, o_ref = rest
    else:
        (o_ref,) = rest
    acc = _dot(a_ref[...], w_ref[...]) * MLA_LOGIT_SCALE
    if rope:
        parts = []
        for h in range(acc.shape[1] // MLA_QK):
            parts.append(acc[:, h * MLA_QK:h * MLA_QK + NOPE_DIM])
            parts.append(_rope_chunks(acc[:, h * MLA_QK + NOPE_DIM:(h + 1) * MLA_QK], cos_ref[...], sin_ref[...]))
        acc = jnp.concatenate(parts, axis=1)
    o_ref[...] = acc.astype(o_ref.dtype)


def _mm_qb(a, w, cos2, sin2, rows_per_batch, rope, bm=1024):
    m, k = a.shape
    n = w.shape[1]
    bm = _tile(rows_per_batch, bm)
    bn = 4 * MLA_QK if n % (4 * MLA_QK) == 0 else MLA_QK
    tpb = rows_per_batch // bm
    in_specs = [_a_spec(bm, k), _w_spec(w, bn)]
    args = [a, w]
    if rope:
        in_specs += [_pos_spec(bm, LANE, tpb), _pos_spec(bm, LANE, tpb)]
        args += [cos2, sin2]
    return _mm_call(functools.partial(_qb_body, rope=rope), (m // bm, n // bn), in_specs, _mn_spec(bm, bn),
                    jax.ShapeDtypeStruct((m, n), BF16), args, "mm_qb")


def _kexp_body(a_ref, w_ref, kr_ref, o_ref):
    acc = _dot(a_ref[...], w_ref[...]).astype(o_ref.dtype)
    kr = kr_ref[...]
    parts = []
    for h in range(acc.shape[1] // NOPE_DIM):
        parts += [acc[:, h * NOPE_DIM:(h + 1) * NOPE_DIM], kr]
    o_ref[...] = jnp.concatenate(parts, axis=1)


def _vexp_body(a_ref, w_ref, o_ref):
    acc = _dot(a_ref[...], w_ref[...]).astype(o_ref.dtype)
    lane = lax.broadcasted_iota(jnp.int32, (acc.shape[0], V_DIM), 1)
    ones = jnp.where(lane == 0, 1.0, 0.0).astype(o_ref.dtype)
    parts = []
    for h in range(acc.shape[1] // V_DIM):
        parts += [acc[:, h * V_DIM:(h + 1) * V_DIM], ones]
    o_ref[...] = jnp.concatenate(parts, axis=1)


def _kvexp_body(a_ref, wk_ref, wv_ref, kr_ref, k_ref, v_ref):
    _kexp_body(a_ref, wk_ref, kr_ref, k_ref)
    _vexp_body(a_ref, wv_ref, v_ref)


def _mm_kvexp(ckv, w_uk, w_uv, kr, bm=1536, heads_per_tile=8):
    m, k = ckv.shape
    heads = w_uk.shape[1] // NOPE_DIM
    bm = _tile(m, bm)
    hpt = heads_per_tile if heads % heads_per_tile == 0 else 1
    return _mm_call(_kvexp_body, (m // bm, heads // hpt),
                    [_a_spec(bm, k), _w_spec(w_uk, hpt * NOPE_DIM), _w_spec(w_uv, hpt * V_DIM),
                     pl.BlockSpec((bm, LANE), lambda i, j: (i, 0))],
                    [_mn_spec(bm, hpt * MLA_QK), _mn_spec(bm, 2 * hpt * V_DIM)],
                    [jax.ShapeDtypeStruct((m, heads * MLA_QK), BF16), jax.ShapeDtypeStruct((m, 2 * heads * V_DIM), BF16)],
                    (ckv, w_uk, w_uv, kr), "mm_kvexp")


def _merge_body(oa_ref, ob_ref, oc_ref, wa_ref, wb_ref, wc_ref, ga_ref, gb_ref, gc_ref, o_ref):
    y = ga_ref[...] * _dot(oa_ref[...], wa_ref[...])
    y = y + gb_ref[...] * _dot(ob_ref[...], wb_ref[...])
    y = y + gc_ref[...] * _dot(oc_ref[...], wc_ref[...])
    o_ref[...] = y.astype(o_ref.dtype)


def _mm_merge(o_a, o_b, o_c, w_br, layer, gates, bm=1024, bn=256):
    m, k = o_a.shape
    n = w_br.shape[3]
    bm, bn = _tile(m, bm), _tile(n, bn)
    nb = n // bn
    wspec = [pl.BlockSpec((None, None, k, bn), (lambda i, j, r=r: (layer, r, 0, j))) for r in range(N_BRANCH)]
    gspec = [pl.BlockSpec((bm, bn), (lambda i, j, r=r: (i, r * nb + j))) for r in range(N_BRANCH)]
    return _mm_call(_merge_body, (m // bm, nb), [_a_spec(bm, k)] * 3 + wspec + gspec, _mn_spec(bm, bn),
                    jax.ShapeDtypeStruct((m, n), BF16), (o_a, o_b, o_c, w_br, w_br, w_br, gates, gates, gates),
                    "mm_merge")


def _resid_body(a_ref, w_ref, x_ref, gt_ref, o_ref):
    o_ref[...] = x_ref[...] + gt_ref[...] * _dot(a_ref[...], w_ref[...])


def _mm_resid(a, w, x2, gate, rows_per_batch, bm=1024, bn=512, name="mm_resid"):
    m, k = a.shape
    n = w.shape[1]
    bm, bn = _tile(rows_per_batch, bm), _tile(n, bn)
    tpb = rows_per_batch // bm
    per_batch = gate.shape[0] > 1
    gspec = pl.BlockSpec((None, 1, bn), (lambda i, j: (i // tpb, 0, j)) if per_batch else (lambda i, j: (0, 0, j)))
    return _mm_call(_resid_body, (m // bm, n // bn), [_a_spec(bm, k), _w_spec(w, bn), _mn_spec(bm, bn), gspec],
                    _mn_spec(bm, bn), jax.ShapeDtypeStruct((m, n), F32), (a, w, x2, gate), name)


def _ffn_up_body(a_ref, wg_ref, wu_ref, o_ref):
    a = a_ref[...]
    g = _dot(a, wg_ref[...])
    o_ref[...] = (g * _sigmoid(g) * _dot(a, wu_ref[...])).astype(o_ref.dtype)


def _mm_ffn_up(a, wg, wu, bm=2048, bn=256):
    m, k = a.shape
    n = wg.shape[1]
    bm, bn = _tile(m, bm), _tile(n, bn)
    return _mm_call(_ffn_up_body, (m // bm, n // bn), [_a_spec(bm, k), _w_spec(wg, bn), _w_spec(wu, bn)],
                    _mn_spec(bm, bn), jax.ShapeDtypeStruct((m, n), BF16), (a, wg, wu), "mm_ffn_up")


def _with_ones_col(v):
    lane = lax.broadcasted_iota(jnp.int32, v.shape, 1)
    return jnp.concatenate([v, jnp.where(lane == 0, 1.0, 0.0).astype(v.dtype)], axis=1)


def _win_body(sink_ref, *refs, groups, kv_per_step):
    q_refs = refs[:kv_per_step]
    kp_ref, kc_ref, kn_ref, vp_ref, vc_ref, vn_ref, kx_ref, vx_ref, mask_ref, o_ref = refs[kv_per_step:]
    kv0 = pl.program_id(1) * kv_per_step
    mask = mask_ref[...]
    chains = [(j, g) for j in range(kv_per_step) for g in range(groups)]
    k_all, v_all = [], []
    for j in range(kv_per_step):
        c = slice(j * HEAD_DIM, (j + 1) * HEAD_DIM)
        k_all.append(jnp.concatenate([kp_ref[:, c], kc_ref[:, c], kn_ref[:, c], kx_ref[:, c]], axis=0))
        v_all.append(_with_ones_col(jnp.concatenate([vp_ref[:, c], vc_ref[:, c], vn_ref[:, c], vx_ref[:, c]], axis=0)))
    s = [_dot_nt(q_refs[j][:, g * HEAD_DIM:(g + 1) * HEAD_DIM], k_all[j]) + mask for j, g in chains]
    sink = [sink_ref[(kv0 + j) * groups + g] * LOG2E for j, g in chains]
    m = [jnp.maximum(jnp.max(si, axis=1, keepdims=True), sk) for si, sk in zip(s, sink)]
    p = [jnp.exp2(si - mi).astype(BF16) for si, mi in zip(s, m)]
    for i, (j, g) in enumerate(chains):
        acc = _dot(p[i], v_all[j])
        l = acc[:, HEAD_DIM:HEAD_DIM + 1] + jnp.exp2(sink[i] - m[i])
        c0 = (j * groups + g) * HEAD_DIM
        o_ref[:, c0:c0 + HEAD_DIM] = (acc[:, :HEAD_DIM] / l).astype(o_ref.dtype)


def _win_masks(ctx):
    qi = np.arange(BLOCK)[:, None]
    kj = np.arange(3 * BLOCK)[None, :]
    band = np.abs(kj - qi - BLOCK) <= WINDOW
    out = []
    for lo, hi in ((BLOCK, 3 * BLOCK), (0, 3 * BLOCK), (0, 2 * BLOCK)):
        ok = band & (kj >= lo) & (kj < hi)
        out.append(np.concatenate([np.where(ok, 0.0, NEG_INF), np.zeros((BLOCK, ctx))], axis=1))
    return np.stack(out).astype(np.float32)


def _dense_body(*refs, tk, scale, has_sink, ones_col):
    if has_sink:
        sink_ref, q_ref, k_ref, v_ref, o_ref = refs
    else:
        q_ref, k_ref, v_ref, o_ref = refs
    q = q_ref[...]
    lk = k_ref.shape[0]
    dv = o_ref.shape[1]
    exp = jnp.exp2 if scale is None else jnp.exp
    m = l = acc = None
    nchunks = lk // tk

    def logits(c):
        s = _dot_nt(q, k_ref[c * tk:(c + 1) * tk, :])
        return s if scale is None else s * scale

    for c in range(nchunks):
        s = logits(c)
        mc = jnp.max(s, axis=1, keepdims=True)
        if c == 0:
            m_new = mc
            if has_sink:
                sink = sink_ref[pl.program_id(1)]
                m_new = jnp.maximum(m_new, sink)
        else:
            m_new = jnp.maximum(m, mc)
        p = exp(s - m_new)
        pv = _dot(p.astype(BF16), v_ref[c * tk:(c + 1) * tk, :])
        if not ones_col:
            ps = jnp.sum(p, axis=1, keepdims=True)
            if c == 0:
                l = ps + exp(sink - m_new) if has_sink else ps
            else:
                l = exp(m - m_new) * l + ps
        acc = pv if c == 0 else exp(m - m_new) * acc + pv
        m = m_new
    if ones_col:
        l = acc[:, dv:dv + 1]
        acc = acc[:, :dv]
    o_ref[...] = (acc / l).astype(o_ref.dtype)


def _dense_attention(q3, k3, v3, *, heads, groups, dk, dv, q_off, k_off, v_off, lk, tq, tk, scale, sink=None,
                     ones_col=False, name="dense_attn"):
    assert not (ones_col and sink is not None)
    bsz, lq, _ = q3.shape
    tq = _tile(lq, tq)
    dvw = 2 * dv if ones_col else dv
    in_specs = [pl.BlockSpec((None, tq, dk), lambda b, h, i: (b, i, q_off + h)),
                pl.BlockSpec((None, lk, dk), lambda b, h, i: (b, 0, k_off + h // groups)),
                pl.BlockSpec((None, lk, dvw), lambda b, h, i: (b, 0, v_off + h // groups))]
    args = [q3, k3, v3]
    if sink is not None:
        in_specs = [pl.BlockSpec(memory_space=pltpu.SMEM)] + in_specs
        args = [sink] + args
    return pl.pallas_call(
        functools.partial(_dense_body, tk=tk, scale=scale, has_sink=sink is not None, ones_col=ones_col),
        grid=(bsz, heads, lq // tq),
        in_specs=in_specs,
        out_specs=pl.BlockSpec((None, tq, dv), lambda b, h, i: (b, i, h)),
        out_shape=jax.ShapeDtypeStruct((bsz, lq, heads * dv), BF16),
        compiler_params=_params(3),
        name=name,
    )(*args)


def _na_body(q_ref, kp_ref, kc_ref, kn_ref, vp_ref, vc_ref, vn_ref, kx_ref, vx_ref, bias_ref, o_ref):
    heads = range(bias_ref.shape[0])
    cols = [slice(h * HEAD_DIM, (h + 1) * HEAD_DIM) for h in heads]
    s = [_dot_nt(q_ref[:, c], jnp.concatenate([kp_ref[:, c], kc_ref[:, c], kn_ref[:, c], kx_ref[:, c]], axis=0))
         + bias_ref[h] for h, c in zip(heads, cols)]
    p = [jnp.exp2(sh - jnp.max(sh, axis=1, keepdims=True)).astype(BF16) for sh in s]
    for h, c in zip(heads, cols):
        v_all = _with_ones_col(jnp.concatenate([vp_ref[:, c], vc_ref[:, c], vn_ref[:, c], vx_ref[:, c]], axis=0))
        acc = _dot(p[h], v_all)
        o_ref[:, c] = (acc[:, :HEAD_DIM] / acc[:, HEAD_DIM:HEAD_DIM + 1]).astype(o_ref.dtype)


def _na_valid(rows):
    r = NA_QROWS
    nblk = rows // r
    assert rows % r == 0 and nblk >= 3 and rows >= NA_ROWS and NA_ROWS >= 2 * r
    kwin = min(NA_ROWS, rows)
    valid = []
    for i in (0, 1, nblk - 1):
        qrow = (r * i + np.arange(r))[:, None, None, None]
        qcol = np.arange(GRID_W)[None, :, None, None]
        krow = (r * (i - 1) + np.arange(3 * r))[None, None, :, None]
        kcol = np.arange(GRID_W)[None, None, None, :]
        r0 = np.clip(qrow - kwin // 2, 0, rows - kwin)
        c0 = np.clip(qcol - NA_COLS // 2, 0, GRID_W - NA_COLS)
        ok = (krow >= 0) & (krow < rows) & (krow >= r0) & (krow < r0 + kwin) & (kcol >= c0) & (kcol < c0 + NA_COLS)
        valid.append(ok.reshape(r * GRID_W, 3 * r * GRID_W))
    return np.stack(valid)


def _na_bias_tables(rpb, rows, ctx):
    r, w = NA_QROWS, GRID_W
    heads, nro, nco = rpb.shape
    circ = jnp.concatenate([rpb[..., NA_COLS - 1:], jnp.zeros((heads, nro, 2 * w - nco), rpb.dtype),
                            rpb[..., :NA_COLS - 1]], axis=-1)
    toep = jnp.broadcast_to(circ[:, :, None, :], (heads, nro, w, 2 * w)).reshape(heads, nro, 2 * w * w)
    toep = toep[:, :, :w * (2 * w - 1)].reshape(heads, nro, w, 2 * w - 1)[..., :w]
    strips = []
    for qr in range(r):
        lo = NA_ROWS - 1 - r - qr
        strips.append(toep[:, lo:lo + 3 * r].transpose(0, 2, 1, 3).reshape(heads, w, 3 * r * w))
    tile = jnp.concatenate(strips, axis=1) * LOG2E
    local = jnp.where(jnp.asarray(_na_valid(rows))[None], tile[:, None], NEG_INF)
    return jnp.concatenate([local, jnp.zeros(local.shape[:3] + (ctx,), local.dtype)], axis=-1)


def _neighborhood_attention(plain, plain_c, rpb, bsz, seq, q_blk, k_blk, v_blk, heads_per_step):
    rows = seq // GRID_W
    tq = NA_QROWS * GRID_W
    nblk = rows // NA_QROWS
    ctx = plain_c.shape[0] // bsz
    hb = heads_per_step
    hw = hb * HEAD_DIM
    bias = _na_bias_tables(rpb, rows, ctx)

    def blk(off, d):
        def imap(b, h, i):
            return (b * nblk + jnp.clip(i + d, 0, nblk - 1), off + h)
        return pl.BlockSpec((tq, hw), imap)

    def bias_map(b, h, i):
        return (h, jnp.where(i == 0, 0, jnp.where(i == nblk - 1, 2, 1)), 0, 0)

    in_specs = [blk(q_blk, 0),
                blk(k_blk, -1), blk(k_blk, 0), blk(k_blk, 1),
                blk(v_blk, -1), blk(v_blk, 0), blk(v_blk, 1),
                pl.BlockSpec((ctx, hw), lambda b, h, i: (b, k_blk + h)),
                pl.BlockSpec((ctx, hw), lambda b, h, i: (b, v_blk + h)),
                pl.BlockSpec((hb, None, tq, 3 * tq + ctx), bias_map)]
    return pl.pallas_call(
        _na_body,
        grid=(bsz, C_HEADS // hb, nblk),
        in_specs=in_specs,
        out_specs=pl.BlockSpec((tq, hw), lambda b, h, i: (b * nblk + i, h)),
        out_shape=jax.ShapeDtypeStruct((bsz * seq, C_HEADS * HEAD_DIM), BF16),
        compiler_params=_params(3),
        name="na_attn",
    )(plain, plain, plain, plain, plain, plain, plain, plain_c, plain_c, bias)


def _rope_tables(seq):
    t = jnp.arange(seq, dtype=jnp.int32)
    row = (t // GRID_W).astype(F32)
    col = (t % GRID_W).astype(F32)

    def cs(rot_dim):
        n_freq = rot_dim // 4
        inv = jnp.power(ROPE_BASE, -jnp.arange(n_freq, dtype=F32) / n_freq)
        ang = jnp.concatenate([row[:, None] * inv, col[:, None] * inv], axis=-1)
        return jnp.cos(ang), jnp.sin(ang)

    ca, sa = cs(HEAD_DIM)
    cb, sb = cs(ROPE_DIM)
    zb = jnp.zeros_like(cb)
    return (jnp.concatenate([ca, ca], axis=1), jnp.concatenate([-sa, sa], axis=1),
            jnp.concatenate([cb, zb, cb, zb], axis=1), jnp.concatenate([-sb, zb, sb, zb], axis=1))


def _spread_rope_cols(w):
    h = ROPE_DIM // 2
    z = jnp.zeros(w.shape[:-1] + (LANE // 2 - h,), w.dtype)
    return jnp.concatenate([w[..., :h], z, w[..., h:], z], axis=-1)


def _stacked_weights(w_in, w_q_b, w_kv_b, w_br, w_out, w_ffn_gate, w_ffn_up, w_ffn_down):
    nl = w_in.shape[0]
    kvh = A_KV_HEADS * HEAD_DIM
    ch = C_HEADS * HEAD_DIM
    ah = A_HEADS * HEAD_DIM
    sizes = [kvh, kvh, KV_LORA, ROPE_DIM, ch, ch, ah, Q_LORA, ch, N_BRANCH * D_MODEL]
    offs = np.concatenate([[0], np.cumsum(sizes)])
    a_k, a_v, b_ckv, b_kr, c_k, c_v, a_q, b_qa, c_q, gate = [w_in[..., offs[i]:offs[i + 1]]
                                                             for i in range(len(sizes))]
    return {
        "rope": jnp.concatenate([a_k, a_q], axis=-1).astype(BF16),
        "plain": jnp.concatenate([c_k, c_v, c_q, a_v], axis=-1).astype(BF16),
        "ckv": jnp.concatenate([b_ckv, _spread_rope_cols(b_kr)], axis=-1).astype(BF16),
        "bqa": b_qa.astype(BF16),
        "gate": gate.astype(BF16),
        "qb": jnp.concatenate([w_q_b[..., :NOPE_DIM], _spread_rope_cols(w_q_b[..., NOPE_DIM:])], axis=-1)
        .reshape(nl, Q_LORA, B_HEADS * MLA_QK).astype(BF16),
        "uk": w_kv_b[..., :NOPE_DIM].reshape(nl, KV_LORA, B_HEADS * NOPE_DIM).astype(BF16),
        "uv": w_kv_b[..., NOPE_DIM:].reshape(nl, KV_LORA, B_HEADS * V_DIM).astype(BF16),
        "br": w_br.astype(BF16),
        "out": w_out.astype(BF16),
        "ffn_gate": w_ffn_gate.astype(BF16),
        "ffn_up": w_ffn_up.astype(BF16),
        "ffn_down": w_ffn_down.astype(BF16),
    }


def _layer(x2, ctx2, cv, tabs, layer, sw, w_ada, b_ada, g_mix, sink_a, g_q_a, g_kv_a, rpb_c, g_ffn, last):
    bsz, seq, ctx, d = BATCH, SEQ, CTX_LEN, D_MODEL
    cos_a, sin_a, cos_b, sin_b = tabs
    kvb = A_KV_HEADS * HEAD_DIM // LANE
    chb = C_HEADS * HEAD_DIM // LANE
    na_hb = math.gcd(chb, 8)

    mod = _adaln(cv, w_ada, b_ada, layer)
    sh_m, sc_m, gt_m, sh_f, sc_f, gt_f = [mod[:bsz, i * d:(i + 1) * d].reshape(bsz, 1, d) for i in range(6)]
    csh_m, csc_m, cgt_m, csh_f, csc_f, cgt_f = [mod[bsz:bsz + 1, i * d:(i + 1) * d].reshape(1, 1, d) for i in range(6)]
    w = {name: _LayerWeight(arr, layer) for name, arr in sw.items() if name != "br"}

    cm = bsz * ctx
    hc = _norm_mod(ctx2, g_mix, csh_m, csc_m, cm)
    ropecols_c = _mm_plain(hc, w["rope"] if not last else w["rope"].cols(kvb * LANE), bm=cm, name="mm_plain_c")
    plain_c = _mm_plain(hc, w["plain"], bm=cm, name="mm_plain_c")
    ckv_c, kr_c = _mm_ckv(hc, w["ckv"], g_kv_a, None, None, cm, rope=False, bm=cm)

    hx = _norm_mod(x2, g_mix, sh_m, sc_m, seq)
    qscale = jnp.full((1, A_HEADS * HEAD_DIM), HEAD_LOGIT_SCALE, F32)
    cs_rope = jnp.concatenate([jnp.ones((1, kvb * LANE), F32), qscale], axis=1)
    cs_plain = jnp.concatenate([jnp.ones((1, 2 * chb * LANE), F32), jnp.full((1, chb * LANE), HEAD_LOGIT_SCALE, F32),
                                jnp.ones((1, kvb * LANE), F32)], axis=1)
    qk_a = _mm_rope(hx, w["rope"], cs_rope, cos_a, sin_a, seq)
    plain = _mm_plain(hx, w["plain"], cs_plain)
    ckv, kr = _mm_ckv(hx, w["ckv"], g_kv_a, cos_b, sin_b, seq, rope=True)
    bqa = _mm_rms(hx, w["bqa"], g_q_a)
    gates = _mm_sigmoid(hx, w["gate"])
    qb = _mm_qb(bqa, w["qb"], cos_b, sin_b, seq, rope=True)

    lk = ctx + seq
    ckv_all = jnp.concatenate([ckv_c.reshape(bsz, ctx, -1), ckv.reshape(bsz, seq, -1)], axis=1).reshape(bsz * lk, -1)
    kr_all = jnp.concatenate([kr_c.reshape(bsz, ctx, -1), kr.reshape(bsz, seq, -1)], axis=1).reshape(bsz * lk, -1)
    k_exp, v_exp = _mm_kvexp(ckv_all, w["uk"], w["uv"], kr_all)
    k_exp, v_exp = k_exp.reshape(bsz, lk, -1), v_exp.reshape(bsz, lk, -1)

    o_a = _window_attention(sink_a, qk_a, plain, ropecols_c, plain_c, bsz, seq, kvb, 3 * chb)
    o_b = _dense_attention(qb.reshape(bsz, seq, -1), k_exp, v_exp, heads=B_HEADS, groups=1, dk=MLA_QK, dv=V_DIM,
                           q_off=0, k_off=0, v_off=0, lk=lk, tq=1024, tk=_mla_chunk(lk), scale=None, ones_col=True,
                           name="mla_attn").reshape(bsz * seq, -1)
    o_c = _neighborhood_attention(plain, plain_c, rpb_c, bsz, seq, q_blk=2 * chb // na_hb, k_blk=0,
                                  v_blk=chb // na_hb, heads_per_step=na_hb)

    y = _mm_merge(o_a, o_b, o_c, sw["br"], layer, gates)
    x2 = _mm_resid(y, w["out"], x2, gt_m, seq)
    h2 = _norm_mod(x2, g_ffn, sh_f, sc_f, seq)
    act = _mm_ffn_up(h2, w["ffn_gate"], w["ffn_up"])
    x2 = _mm_resid(act, w["ffn_down"], x2, gt_f, seq, bm=512, bn=512, name="mm_ffn_down")

    if not last:
        gates_c = _mm_sigmoid(hc, w["gate"], bm=cm)
        bqa_c = _mm_rms(hc, w["bqa"], g_q_a, bm=cm)
        qb_c = _mm_qb(bqa_c, w["qb"], None, None, cm, rope=False, bm=cm)
        rc3 = ropecols_c.reshape(bsz, ctx, -1)
        pc3 = plain_c.reshape(bsz, ctx, -1)
        oc_a = _dense_attention(rc3, rc3, pc3, heads=A_HEADS, groups=A_HEADS // A_KV_HEADS, dk=HEAD_DIM, dv=HEAD_DIM,
                                q_off=kvb, k_off=0, v_off=3 * chb, lk=ctx, tq=ctx, tk=ctx, scale=HEAD_DIM ** -0.5,
                                sink=sink_a, name="ctx_attn_a").reshape(bsz * ctx, -1)
        oc_b = _dense_attention(qb_c.reshape(bsz, ctx, -1), k_exp, v_exp, heads=B_HEADS, groups=1, dk=MLA_QK,
                                dv=V_DIM, q_off=0, k_off=0, v_off=0, lk=ctx, tq=ctx, tk=ctx, scale=None, ones_col=True,
                                name="ctx_attn_b").reshape(bsz * ctx, -1)
        oc_c = _dense_attention(pc3, pc3, pc3, heads=C_HEADS, groups=1, dk=HEAD_DIM, dv=HEAD_DIM,
                                q_off=2 * chb, k_off=0, v_off=chb, lk=ctx, tq=ctx, tk=ctx,
                                scale=HEAD_DIM ** -0.5, name="ctx_attn_c").reshape(bsz * ctx, -1)
        yc = _mm_merge(oc_a, oc_b, oc_c, sw["br"], layer, gates_c, bm=cm, bn=512)
        ctx2 = _mm_resid(yc, w["out"], ctx2, cgt_m, cm, bm=cm, name="mm_resid_c")
        hc2 = _norm_mod(ctx2, g_ffn, csh_f, csc_f, cm)
        act_c = _mm_ffn_up(hc2, w["ffn_gate"], w["ffn_up"], bm=cm)
        ctx2 = _mm_resid(act_c, w["ffn_down"], ctx2, cgt_f, cm, bm=cm, bn=256, name="mm_ffn_down_c")
    return x2, ctx2


def _mla_chunk(lk):
    for parts in range(40, 0, -1):
        if lk % parts == 0 and (lk // parts) % (2 * LANE) == 0:
            return lk // parts
    return lk


def _window_attention(sink, qk_a, plain, ropecols_c, plain_c, bsz, seq, kvb, v_lane_blk):
    nblk = seq // BLOCK
    groups = A_HEADS // A_KV_HEADS
    gw = groups * HEAD_DIM
    ctx = plain_c.shape[0] // bsz
    kvs = math.gcd(A_KV_HEADS, 4)
    kw = kvs * HEAD_DIM

    voff = v_lane_blk // kvs

    def blk(d, off):
        def imap(b, h, n):
            return (b * nblk + jnp.clip(n + d, 0, nblk - 1), off + h)
        return pl.BlockSpec((BLOCK, kw), imap)

    def qspec(j):
        return pl.BlockSpec((BLOCK, gw), lambda b, h, n: (b * nblk + n, kvb // groups + kvs * h + j))

    kxspec = pl.BlockSpec((ctx, kw), lambda b, h, n: (b, h))
    vxspec = pl.BlockSpec((ctx, kw), lambda b, h, n: (b, voff + h))
    mspec = pl.BlockSpec((None, BLOCK, 3 * BLOCK + ctx),
                         lambda b, h, n: (jnp.where(n == 0, 0, jnp.where(n == nblk - 1, 2, 1)), 0, 0))
    in_specs = ([pl.BlockSpec(memory_space=pltpu.SMEM)] + [qspec(j) for j in range(kvs)]
                + [blk(-1, 0), blk(0, 0), blk(1, 0), blk(-1, voff), blk(0, voff), blk(1, voff), kxspec, vxspec, mspec])
    return pl.pallas_call(
        functools.partial(_win_body, groups=groups, kv_per_step=kvs),
        grid=(bsz, A_KV_HEADS // kvs, nblk),
        in_specs=in_specs,
        out_specs=pl.BlockSpec((BLOCK, kvs * gw), lambda b, h, n: (b * nblk + n, h)),
        out_shape=jax.ShapeDtypeStruct((bsz * seq, A_HEADS * HEAD_DIM), BF16),
        compiler_params=_params(3),
        name="win_attn",
    )(sink, *([qk_a] * kvs), qk_a, qk_a, qk_a, plain, plain, plain, ropecols_c, plain_c,
      jnp.asarray(_win_masks(ctx)))


def kernel(x, c, ctx, c_ctx, w_ada, b_ada, g_mix, w_in, sink_a, g_q_a, w_q_b, g_kv_a, w_kv_b, rpb_c, w_br, w_out,
           g_ffn, w_ffn_gate, w_ffn_up, w_ffn_down, g_final):
    bsz, seq, d = x.shape
    tabs = _rope_tables(seq)
    pad = (-(bsz + 1)) % 8
    cv = jnp.concatenate([c, c_ctx[None, :], jnp.zeros((pad, d), F32)], axis=0)
    x2 = x.reshape(bsz * seq, d)
    ctx2 = ctx.reshape(bsz * ctx.shape[1], d)
    sw = _stacked_weights(w_in, w_q_b, w_kv_b, w_br, w_out, w_ffn_gate, w_ffn_up, w_ffn_down)
    for l in range(DEPTH):
        x2, ctx2 = _layer(x2, ctx2, cv, tabs, l, sw, w_ada, b_ada, g_mix[l], sink_a[l], g_q_a[l], g_kv_a[l], rpb_c[l],
                          g_ffn[l], last=(l == DEPTH - 1))
    zero = jnp.zeros((1, 1, d), F32)
    out = _norm_mod(x2, g_final, zero, zero, seq, out_dtype=F32)
    return out.reshape(bsz, seq, d)
```
